```python
import jax
import jax.numpy as jnp
from jax import lax
import numpy as np

D_MODEL = 2048
BATCH = 4
SEQ = 4096
DEPTH = 1

MIX_WIDTH = 2 * D_MODEL
FOURIER_WIDTH = MIX_WIDTH // 4
FOURIER_GROUPS = 8
FOURIER_GROUP_DIM = FOURIER_WIDTH // FOURIER_GROUPS
SSM_WIDTH = MIX_WIDTH - FOURIER_WIDTH
SSM_HEAD_DIM = 64
SSM_HEADS = SSM_WIDTH // SSM_HEAD_DIM
SSM_GROUPS = 8
HEADS_PER_GROUP = SSM_HEADS // SSM_GROUPS
SSM_STATE = 128
SSM_CHUNK = 128
SSM_CONV = 5
XBC_WIDTH = SSM_WIDTH + 2 * SSM_GROUPS * SSM_STATE
IN_PROJ_WIDTH = FOURIER_WIDTH + SSM_WIDTH + XBC_WIDTH + SSM_HEADS
FFN_DIM = ((8 * D_MODEL // 3 + 255) // 256) * 256
FFN_CONV = 3
NORM_EPS = 1e-5
DT_MIN = 1e-3
DT_MAX = 1e-1

kernel_name = "fnet_bissd_hymba_convffn_encoder"


def rms_norm(x, gain):
    xf = x.astype(jnp.float32)
    var = jnp.mean(xf * xf, axis=-1, keepdims=True)
    return (xf * lax.rsqrt(var + NORM_EPS) * gain.astype(jnp.float32)).astype(x.dtype)


def depthwise_conv(x, w, b):
    width = w.shape[0]
    pad = width // 2
    y = lax.conv_general_dilated(
        x, w[:, None, :].astype(x.dtype), window_strides=(1,), padding=((pad, pad),),
        dimension_numbers=("NWC", "WIO", "NWC"), feature_group_count=x.shape[-1])
    return y + b.astype(y.dtype)


def fourier_mixer(u, w_mix):
    b, s, _ = u.shape
    ug = u.reshape(b, s, FOURIER_GROUPS, FOURIER_GROUP_DIM).astype(jnp.float32)
    f = jnp.fft.fftn(ug, axes=(1, 3), norm="ortho").real
    out = jnp.einsum("bsgc,gcd->bsgd", f, w_mix.astype(jnp.float32))
    return out.reshape(b, s, FOURIER_WIDTH).astype(u.dtype)


def ssd_scan(x, dt, a, bmat, cmat):
    b, s, h, p = x.shape
    nc = s // SSM_CHUNK
    g, r, n = SSM_GROUPS, HEADS_PER_GROUP, SSM_STATE
    xd = (x * dt[..., None]).reshape(b, nc, SSM_CHUNK, g, r, p)
    adt = (dt * a).reshape(b, nc, SSM_CHUNK, g, r)
    bc = bmat.reshape(b, nc, SSM_CHUNK, g, n)
    cc = cmat.reshape(b, nc, SSM_CHUNK, g, n)
    a_cum = jnp.cumsum(adt, axis=2)
    a_cum_t = jnp.moveaxis(a_cum, 2, -1)
    seg = a_cum_t[..., :, None] - a_cum_t[..., None, :]
    lower = jnp.tril(jnp.ones((SSM_CHUNK, SSM_CHUNK), dtype=bool))
    decay = jnp.exp(jnp.where(lower, seg, -jnp.inf))
    cb = jnp.einsum("bclgn,bcsgn->bcgls", cc, bc)
    y_diag = jnp.einsum("bcgls,bcgrls,bcsgrp->bclgrp", cb, decay, xd)
    decay_to_end = jnp.exp(a_cum[:, :, -1:] - a_cum)
    states = jnp.einsum("bclgn,bclgr,bclgrp->bcgrpn", bc, decay_to_end, xd)
    chunk_decay = jnp.exp(a_cum[:, :, -1])

    def step(carry, inp):
        st, dec = inp
        return carry * dec[..., None, None] + st, carry

    init = jnp.zeros((b, g, r, p, n), jnp.float32)
    _, prev = lax.scan(step, init, (jnp.moveaxis(states, 1, 0), jnp.moveaxis(chunk_decay, 1, 0)))
    prev = jnp.moveaxis(prev, 0, 1)
    y_off = jnp.einsum("bclgn,bcgrpn,bclgr->bclgrp", cc, prev, jnp.exp(a_cum))
    return (y_diag + y_off).reshape(b, s, h, p)


def bidirectional_ssd_mixer(z, xbc, dt_raw, conv_w, conv_b, dt_bias_fwd, a_log_fwd,
                            dt_bias_bwd, a_log_bwd, d_skip, norm_w):
    b, s, _ = z.shape
    gn = SSM_GROUPS * SSM_STATE
    xbc = jax.nn.silu(depthwise_conv(xbc, conv_w, conv_b)).astype(jnp.float32)
    xs = xbc[..., :SSM_WIDTH].reshape(b, s, SSM_HEADS, SSM_HEAD_DIM)
    bm = xbc[..., SSM_WIDTH:SSM_WIDTH + gn].reshape(b, s, SSM_GROUPS, SSM_STATE)
    cm = xbc[..., SSM_WIDTH + gn:].reshape(b, s, SSM_GROUPS, SSM_STATE)
    dt_raw = dt_raw.astype(jnp.float32)
    dt_f = jax.nn.softplus(dt_raw + dt_bias_fwd.astype(jnp.float32))
    dt_b = jax.nn.softplus(dt_raw + dt_bias_bwd.astype(jnp.float32))
    a_f = -jnp.exp(a_log_fwd.astype(jnp.float32))
    a_b = -jnp.exp(a_log_bwd.astype(jnp.float32))
    flip = lambda t: jnp.flip(t, axis=1)
    y_f = ssd_scan(xs, dt_f, a_f, bm, cm)
    y_b = flip(ssd_scan(flip(xs), flip(dt_b), a_b, flip(bm), flip(cm)))
    y = y_f + y_b + d_skip.astype(jnp.float32)[:, None] * xs
    y = y.reshape(b, s, SSM_WIDTH) * jax.nn.silu(z.astype(jnp.float32))
    yg = y.reshape(b, s, SSM_GROUPS, SSM_WIDTH // SSM_GROUPS)
    yg = yg * lax.rsqrt(jnp.mean(yg * yg, axis=-1, keepdims=True) + NORM_EPS)
    y = yg.reshape(b, s, SSM_WIDTH) * norm_w.astype(jnp.float32)
    return y.astype(z.dtype)


def setup_inputs(seed: int = 0) -> dict:
    key = jax.random.key(seed)
    ks = jax.random.split(key, 20)
    L = DEPTH
    f32 = jnp.float32
    x = jax.random.normal(ks[0], (BATCH, SEQ, D_MODEL), f32)
    norm_mix_w = 1.0 + 0.02 * jax.random.normal(ks[1], (L, D_MODEL), f32)
    w_in = jax.random.normal(ks[2], (L, D_MODEL, IN_PROJ_WIDTH), f32) * D_MODEL ** -0.5
    fourier_w = jax.random.normal(ks[3], (L, FOURIER_GROUPS, FOURIER_GROUP_DIM, FOURIER_GROUP_DIM), f32) * FOURIER_GROUP_DIM ** -0.5
    ssm_conv_w = jax.random.normal(ks[4], (L, SSM_CONV, XBC_WIDTH), f32) * SSM_CONV ** -0.5
    ssm_conv_b = 0.01 * jax.random.normal(ks[5], (L, XBC_WIDTH), f32)
    u_f = jax.random.uniform(ks[6], (L, SSM_HEADS), f32)
    dt_f = jnp.exp(u_f * (np.log(DT_MAX) - np.log(DT_MIN)) + np.log(DT_MIN))
    dt_bias_fwd = dt_f + jnp.log(-jnp.expm1(-dt_f))
    u_b = jax.random.uniform(ks[7], (L, SSM_HEADS), f32)
    dt_b = jnp.exp(u_b * (np.log(DT_MAX) - np.log(DT_MIN)) + np.log(DT_MIN))
    dt_bias_bwd = dt_b + jnp.log(-jnp.expm1(-dt_b))
    a_log_fwd = jnp.log(jax.random.uniform(ks[8], (L, SSM_HEADS), f32, 1.0, 16.0))
    a_log_bwd = jnp.log(jax.random.uniform(ks[9], (L, SSM_HEADS), f32, 1.0, 16.0))
    ssm_d = 1.0 + 0.02 * jax.random.normal(ks[10], (L, SSM_HEADS), f32)
    ssm_norm_w = 1.0 + 0.02 * jax.random.normal(ks[11], (L, SSM_WIDTH), f32)
    w_out = jax.random.normal(ks[12], (L, MIX_WIDTH, D_MODEL), f32) * MIX_WIDTH ** -0.5
    norm_ffn_w = 1.0 + 0.02 * jax.random.normal(ks[13], (L, D_MODEL), f32)
    w_up = jax.random.normal(ks[14], (L, D_MODEL, 2 * FFN_DIM), f32) * D_MODEL ** -0.5
    ffn_conv_w = jax.random.normal(ks[15], (L, FFN_CONV, 2 * FFN_DIM), f32) * FFN_CONV ** -0.5
    ffn_conv_b = 0.01 * jax.random.normal(ks[16], (L, 2 * FFN_DIM), f32)
    w_down = jax.random.normal(ks[17], (L, FFN_DIM, D_MODEL), f32) * FFN_DIM ** -0.5
    norm_final_w = 1.0 + 0.02 * jax.random.normal(ks[18], (D_MODEL,), f32)
    return {"x": x, "norm_mix_w": norm_mix_w, "w_in": w_in, "fourier_w": fourier_w,
            "ssm_conv_w": ssm_conv_w, "ssm_conv_b": ssm_conv_b,
            "dt_bias_fwd": dt_bias_fwd, "a_log_fwd": a_log_fwd,
            "dt_bias_bwd": dt_bias_bwd, "a_log_bwd": a_log_bwd,
            "ssm_d": ssm_d, "ssm_norm_w": ssm_norm_w, "w_out": w_out,
            "norm_ffn_w": norm_ffn_w, "w_up": w_up, "ffn_conv_w": ffn_conv_w,
            "ffn_conv_b": ffn_conv_b, "w_down": w_down, "norm_final_w": norm_final_w}


def reference(x, norm_mix_w, w_in, fourier_w, ssm_conv_w, ssm_conv_b, dt_bias_fwd, a_log_fwd,
              dt_bias_bwd, a_log_bwd, ssm_d, ssm_norm_w, w_out, norm_ffn_w, w_up,
              ffn_conv_w, ffn_conv_b, w_down, norm_final_w):
    o_z = FOURIER_WIDTH
    o_x = o_z + SSM_WIDTH
    o_dt = o_x + XBC_WIDTH
    for l in range(DEPTH):
        h = rms_norm(x, norm_mix_w[l])
        proj = h @ w_in[l].astype(h.dtype)
        u = proj[..., :o_z]
        z = proj[..., o_z:o_x]
        xbc = proj[..., o_x:o_dt]
        dt_raw = proj[..., o_dt:]
        a_out = fourier_mixer(u, fourier_w[l])
        b_out = bidirectional_ssd_mixer(z, xbc, dt_raw, ssm_conv_w[l], ssm_conv_b[l],
                                        dt_bias_fwd[l], a_log_fwd[l], dt_bias_bwd[l],
                                        a_log_bwd[l], ssm_d[l], ssm_norm_w[l])
        mixed = jnp.concatenate([a_out, b_out.astype(a_out.dtype)], axis=-1)
        x = x + (mixed @ w_out[l].astype(mixed.dtype)).astype(x.dtype)
        h = rms_norm(x, norm_ffn_w[l])
        up = depthwise_conv(h @ w_up[l].astype(h.dtype), ffn_conv_w[l], ffn_conv_b[l])
        gate, val = up[..., :FFN_DIM], up[..., FFN_DIM:]
        x = x + ((jax.nn.silu(gate) * val) @ w_down[l].astype(up.dtype)).astype(x.dtype)
    return rms_norm(x, norm_final_w)
```

```python
import functools

import jax
import jax.numpy as jnp
import numpy as np
from jax import lax
from jax.experimental import pallas as pl
from jax.experimental.pallas import tpu as pltpu

NORM_EPS = 1e-5
FOURIER_GROUPS = 8
SSM_GROUPS = 8
HEADS_PER_GROUP = 6
HEAD_ROWS = 8
SSM_HEAD_DIM = 64
SSM_STATE = 128
CHUNK = 128
GROUP_X = HEADS_PER_GROUP * SSM_HEAD_DIM
GROUP_XBC = GROUP_X + 2 * SSM_STATE
GROUP_COLS = GROUP_X + GROUP_XBC
HALO = 16
VMEM_LIMIT = 56 * 1024 * 1024

F32 = jnp.float32
BF16 = jnp.bfloat16


def _params(semantics):
    return pltpu.CompilerParams(dimension_semantics=semantics, vmem_limit_bytes=VMEM_LIMIT)


def _silu(v):
    return v / (1.0 + jnp.exp(-v))


def _in_proj_kernel(x_ref, g_ref, w_ref, wdt_ref, o_ref, dt_ref, h_ref, *, sub):
    @pl.when(pl.program_id(1) == 0)
    def _():
        def body(r, carry):
            r0 = pl.multiple_of(r * sub, sub)
            xs = x_ref[pl.ds(r0, sub), :]
            var = jnp.mean(xs * xs, axis=-1, keepdims=True)
            hs = xs * lax.rsqrt(var + NORM_EPS) * g_ref[...]
            h_ref[pl.ds(r0, sub), :] = hs.astype(BF16)
            dt_ref[:, pl.ds(r0, sub)] = lax.dot_general(
                wdt_ref[...], hs, (((1,), (1,)), ((), ())),
                precision=lax.Precision.HIGHEST, preferred_element_type=F32)
            return carry
        lax.fori_loop(0, x_ref.shape[0] // sub, body, 0)

    o_ref[...] = jnp.dot(h_ref[...], w_ref[...], preferred_element_type=F32).astype(o_ref.dtype)


def _in_proj(x2, gain, w, wdt_t, *, tm, tn):
    t, d = x2.shape
    n = w.shape[1]
    nh = wdt_t.shape[0]
    return pl.pallas_call(
        functools.partial(_in_proj_kernel, sub=128),
        grid=(t // tm, n // tn),
        in_specs=[
            pl.BlockSpec((tm, d), lambda i, j: (i, 0)),
            pl.BlockSpec((1, d), lambda i, j: (0, 0)),
            pl.BlockSpec((d, tn), lambda i, j: (0, j)),
            pl.BlockSpec((nh, d), lambda i, j: (0, 0)),
        ],
        out_specs=[
            pl.BlockSpec((tm, tn), lambda i, j: (i, j)),
            pl.BlockSpec((nh, tm), lambda i, j: (0, i)),
        ],
        out_shape=[jax.ShapeDtypeStruct((t, n), BF16), jax.ShapeDtypeStruct((nh, t), F32)],
        scratch_shapes=[pltpu.VMEM((tm, d), BF16)],
        compiler_params=_params(("parallel", "arbitrary")),
        name="in_proj",
    )(x2, gain, w, wdt_t)


def _fourier_kernel(cos_ref, sin_ref, u_ref, cc_ref, sc_ref, w_ref, o_ref, *, scale):
    u = u_ref[...]
    uc = jnp.dot(cos_ref[...], u, preferred_element_type=F32)
    us = jnp.dot(sin_ref[...], u, preferred_element_type=F32)
    c = cc_ref.shape[0]
    for g in range(w_ref.shape[0]):
        ucg = uc[:, g * c:(g + 1) * c].astype(BF16)
        usg = us[:, g * c:(g + 1) * c].astype(BF16)
        re = (jnp.dot(ucg, cc_ref[...], preferred_element_type=F32)
              - jnp.dot(usg, sc_ref[...], preferred_element_type=F32)) * scale
        o_ref[:, g * c:(g + 1) * c] = jnp.dot(
            re.astype(BF16), w_ref[g], preferred_element_type=F32).astype(o_ref.dtype)


def _fourier(cos_m, sin_m, proj, cc, sc, w_mix, *, batch, seq, tm):
    g, c, _ = w_mix.shape
    width = g * c
    scale = float(1.0 / np.sqrt(seq * c))
    nt = seq // tm
    return pl.pallas_call(
        functools.partial(_fourier_kernel, scale=scale),
        grid=(batch, nt),
        in_specs=[
            pl.BlockSpec((tm, seq), lambda b, i: (i, 0)),
            pl.BlockSpec((tm, seq), lambda b, i: (i, 0)),
            pl.BlockSpec((seq, width), lambda b, i: (b, 0)),
            pl.BlockSpec((c, c), lambda b, i: (0, 0)),
            pl.BlockSpec((c, c), lambda b, i: (0, 0)),
            pl.BlockSpec((g, c, c), lambda b, i: (0, 0, 0)),
        ],
        out_specs=pl.BlockSpec((tm, width), lambda b, i: (b * nt + i, 0)),
        out_shape=jax.ShapeDtypeStruct((batch * seq, width), BF16),
        compiler_params=_params(("parallel", "arbitrary")),
        name="fourier",
    )(cos_m, sin_m, proj, cc, sc, w_mix)


def _dft_tables(n, split):
    hi = n // split
    s = jnp.arange(n, dtype=jnp.int32)[None, :]
    k1 = jnp.arange(hi, dtype=jnp.int32)[:, None]
    k0 = jnp.arange(split, dtype=jnp.int32)[:, None]
    a1 = ((k1 * s * split) % n).astype(F32) * (2.0 * np.pi / n)
    a0 = ((k0 * s) % n).astype(F32) * (2.0 * np.pi / n)
    c1, s1 = jnp.cos(a1)[:, None, :], jnp.sin(a1)[:, None, :]
    c0, s0 = jnp.cos(a0)[None, :, :], jnp.sin(a0)[None, :, :]
    cos_m = (c1 * c0 - s1 * s0).reshape(n, n)
    sin_m = (s1 * c0 + c1 * s0).reshape(n, n)
    return cos_m, sin_m


COL_P, COL_YS, COL_SW = 0, HEAD_ROWS, 2 * HEAD_ROWS
COL_DIR = 3 * HEAD_ROWS
ROW_P, ROW_DT, ROW_CD = 0, HEAD_ROWS, 2 * HEAD_ROWS
ROW_DIR = 3 * HEAD_ROWS


def _softplus(v):
    return jnp.maximum(v, 0.0) + jnp.log1p(jnp.exp(-jnp.abs(v)))


def _expand_heads(table, off, lane_lo):
    tiles = []
    for j in range(HEADS_PER_GROUP // 2):
        a = jnp.broadcast_to(table[:, off + 2 * j:off + 2 * j + 1], (CHUNK, 128))
        b = jnp.broadcast_to(table[:, off + 2 * j + 1:off + 2 * j + 2], (CHUNK, 128))
        tiles.append(jnp.where(lane_lo, a, b))
    return jnp.concatenate(tiles, axis=1)


def _ssd_kernel(in_ref, dt_ref, cw_ref, cb_ref, hp_ref, d_ref, nw_ref, o_ref,
                xc_ref, bt_ref, colq_ref, rowq_ref, y_ref, st_ref):
    seq = in_ref.shape[0]
    nchunks = seq // CHUNK
    li = lax.broadcasted_iota(jnp.int32, (CHUNK, CHUNK), 0)
    si = lax.broadcasted_iota(jnp.int32, (CHUNK, CHUNK), 1)
    upper = (li <= si)
    upper_f = jnp.where(upper, 1.0, 0.0).astype(F32)
    lane_lo = lax.broadcasted_iota(jnp.int32, (CHUNK, 128), 1) < SSM_HEAD_DIM
    lane_lo_row = lax.broadcasted_iota(jnp.int32, (1, 128), 1) < SSM_HEAD_DIM

    hp = hp_ref[...]
    bias_f, a_f = hp[:, 0:1], -jnp.exp(hp[:, 1:2])
    bias_b, a_b = hp[:, 2:3], -jnp.exp(hp[:, 3:4])

    def prep(c, carry):
        r0 = pl.multiple_of(c * CHUNK, CHUNK)
        cur = in_ref[pl.ds(r0, CHUNK), GROUP_X:].astype(F32)
        p0 = pl.multiple_of(jnp.maximum(r0 - HALO, 0), HALO)
        n0 = pl.multiple_of(jnp.minimum(r0 + CHUNK, seq - HALO), HALO)
        prev = in_ref[pl.ds(p0, HALO), GROUP_X:].astype(F32)
        nxt = in_ref[pl.ds(n0, HALO), GROUP_X:].astype(F32)
        prev = jnp.where(c > 0, prev, 0.0)
        nxt = jnp.where(c < nchunks - 1, nxt, 0.0)
        win = jnp.concatenate([prev, cur, nxt], axis=0)
        rows = CHUNK + 2 * HALO
        width = cw_ref.shape[0]
        acc = jnp.zeros((CHUNK, GROUP_XBC), F32) + cb_ref[...]
        for k in range(width):
            shift = (width // 2 - k) % rows
            tap = win if shift == 0 else pltpu.roll(win, shift, axis=0)
            acc = acc + tap[HALO:HALO + CHUNK, :] * cw_ref[k:k + 1, :]
        act = _silu(acc)
        xc_ref[pl.ds(r0, CHUNK), :] = act.astype(BF16)
        bt_ref[:, pl.ds(r0, CHUNK)] = jnp.transpose(act[:, GROUP_X:GROUP_X + SSM_STATE]).astype(BF16)

        dtr = dt_ref[:, pl.ds(r0, CHUNK)]
        dt_f = _softplus(dtr + bias_f)
        dt_b = _softplus(dtr + bias_b)
        adt_f = dt_f * a_f
        adt_b = dt_b * a_b
        cum_f = jnp.dot(adt_f, upper_f, precision=lax.Precision.HIGHEST, preferred_element_type=F32)
        cum_b = jnp.dot(adt_b, upper_f, precision=lax.Precision.HIGHEST, preferred_element_type=F32)
        tot_f = cum_f[:, CHUNK - 1:CHUNK]
        tot_b = cum_b[:, CHUNK - 1:CHUNK]
        ex_b = cum_b - adt_b
        zeros = jnp.zeros((CHUNK - 2 * COL_DIR, CHUNK), F32)
        table = jnp.concatenate([
            cum_f, jnp.exp(cum_f), jnp.exp(tot_f - cum_f) * dt_f,
            -ex_b, jnp.exp(tot_b - ex_b), jnp.exp(ex_b) * dt_b, zeros], axis=0)
        colq_ref[pl.ds(r0, CHUNK), :] = jnp.transpose(table)
        rowq_ref[:, pl.ds(r0, CHUNK)] = jnp.concatenate([
            cum_f, dt_f, jnp.broadcast_to(jnp.exp(tot_f), (HEAD_ROWS, CHUNK)),
            -ex_b, dt_b, jnp.broadcast_to(jnp.exp(tot_b), (HEAD_ROWS, CHUNK))], axis=0)
        return carry

    lax.fori_loop(0, nchunks, prep, 0)

    def chunk(c, direction):
        r0 = pl.multiple_of(c * CHUNK, CHUNK)
        xcb = xc_ref[pl.ds(r0, CHUNK), :]
        xs = xcb[:, :GROUP_X]
        bm_t = bt_ref[:, pl.ds(r0, CHUNK)]
        cm = xcb[:, GROUP_X + SSM_STATE:]
        colq = colq_ref[pl.ds(r0, CHUNK), :]
        rowq = rowq_ref[pl.ds(direction * ROW_DIR, ROW_DIR), pl.ds(r0, CHUNK)]
        coff = direction * COL_DIR
        cb = jnp.dot(cm, bm_t, preferred_element_type=F32)
        mask = (li >= si) if direction == 0 else (si >= li)

        state = st_ref[...]
        y = jnp.dot(cm, state.astype(BF16), preferred_element_type=F32)
        y = y * _expand_heads(colq, coff + COL_YS, lane_lo)

        xs_lo = jnp.where(lane_lo_row, 1.0, 0.0).astype(BF16)
        tiles = []
        for j in range(HEADS_PER_GROUP // 2):
            xp = xs[:, 128 * j:128 * (j + 1)]
            acc = None
            for half, xm in ((0, xp * xs_lo), (1, xp * (1.0 - xs_lo).astype(BF16))):
                r = 2 * j + half
                pcol = colq[:, coff + COL_P + r:coff + COL_P + r + 1]
                prow = rowq[ROW_P + r:ROW_P + r + 1, :]
                dtrow = rowq[ROW_DT + r:ROW_DT + r + 1, :]
                decay = jnp.exp(jnp.where(mask, pcol - prow, -jnp.inf))
                m = (cb * decay * dtrow).astype(BF16)
                d = jnp.dot(m, xm, preferred_element_type=F32)
                acc = d if acc is None else acc + d
            tiles.append(acc)
        y = y + jnp.concatenate(tiles, axis=1)

        xw = (xs.astype(F32) * _expand_heads(colq, coff + COL_SW, lane_lo)).astype(BF16)
        new = jnp.dot(bm_t, xw, preferred_element_type=F32)
        cd_tiles = []
        for j in range(HEADS_PER_GROUP // 2):
            a = rowq[ROW_CD + 2 * j:ROW_CD + 2 * j + 1, :]
            b = rowq[ROW_CD + 2 * j + 1:ROW_CD + 2 * j + 2, :]
            cd_tiles.append(jnp.where(lane_lo_row, a, b))
        st_ref[...] = state * jnp.concatenate(cd_tiles, axis=1) + new
        return r0, xs, y

    st_ref[...] = jnp.zeros_like(st_ref)

    def fwd(c, carry):
        r0, _, y = chunk(c, 0)
        y_ref[pl.ds(r0, CHUNK), :] = y
        return carry

    lax.fori_loop(0, nchunks, fwd, 0)

    st_ref[...] = jnp.zeros_like(st_ref)

    def bwd(i, carry):
        r0, xs, y = chunk(nchunks - 1 - i, 1)
        y = y + y_ref[pl.ds(r0, CHUNK), :] + d_ref[...] * xs.astype(F32)
        y = y * _silu(in_ref[pl.ds(r0, CHUNK), :GROUP_X].astype(F32))
        ms = jnp.mean(y * y, axis=-1, keepdims=True)
        o_ref[pl.ds(r0, CHUNK), :] = (y * lax.rsqrt(ms + NORM_EPS) * nw_ref[...]).astype(o_ref.dtype)
        return carry

    lax.fori_loop(0, nchunks, bwd, 0)


def _ssd(proj, dt_t, conv_w, conv_b, head_params, d_exp, norm_w, *, batch, seq):
    g = SSM_GROUPS
    width = conv_w.shape[1]
    return pl.pallas_call(
        _ssd_kernel,
        grid=(batch, g),
        in_specs=[
            pl.BlockSpec((seq, GROUP_COLS), lambda b, k: (b, k + 1)),
            pl.BlockSpec((HEAD_ROWS, seq), lambda b, k: (k, b)),
            pl.BlockSpec((None, width, GROUP_XBC), lambda b, k: (k, 0, 0)),
            pl.BlockSpec((None, 1, GROUP_XBC), lambda b, k: (k, 0, 0)),
            pl.BlockSpec((None, HEAD_ROWS, 4), lambda b, k: (k, 0, 0)),
            pl.BlockSpec((None, 1, GROUP_X), lambda b, k: (k, 0, 0)),
            pl.BlockSpec((None, 1, GROUP_X), lambda b, k: (k, 0, 0)),
        ],
        out_specs=pl.BlockSpec((seq, GROUP_X), lambda b, k: (b, k)),
        out_shape=jax.ShapeDtypeStruct((batch * seq, g * GROUP_X), BF16),
        scratch_shapes=[
            pltpu.VMEM((seq, GROUP_XBC), BF16),
            pltpu.VMEM((SSM_STATE, seq), BF16),
            pltpu.VMEM((seq, CHUNK), F32),
            pltpu.VMEM((2 * ROW_DIR, seq), F32),
            pltpu.VMEM((seq, GROUP_X), F32),
            pltpu.VMEM((SSM_STATE, GROUP_X), F32),
        ],
        compiler_params=_params(("parallel", "arbitrary")),
        name="ssd",
    )(proj, dt_t, conv_w, conv_b, head_params, d_exp, norm_w)


def _out_proj_kernel(x_ref, a_ref, b_ref, wa_ref, wb_ref, o_ref):
    acc = jnp.dot(a_ref[...], wa_ref[...], preferred_element_type=F32)
    acc = acc + jnp.dot(b_ref[...], wb_ref[...], preferred_element_type=F32)
    o_ref[...] = x_ref[...] + acc


def _out_proj(x2, a_out, b_out, w_a, w_b, *, tm, tn):
    t, d = x2.shape
    ka, kb = a_out.shape[1], b_out.shape[1]
    return pl.pallas_call(
        _out_proj_kernel,
        grid=(t // tm, d // tn),
        in_specs=[
            pl.BlockSpec((tm, tn), lambda i, j: (i, j)),
            pl.BlockSpec((tm, ka), lambda i, j: (i, 0)),
            pl.BlockSpec((tm, kb), lambda i, j: (i, 0)),
            pl.BlockSpec((ka, tn), lambda i, j: (0, j)),
            pl.BlockSpec((kb, tn), lambda i, j: (0, j)),
        ],
        out_specs=pl.BlockSpec((tm, tn), lambda i, j: (i, j)),
        out_shape=jax.ShapeDtypeStruct((t, d), F32),
        compiler_params=_params(("parallel", "arbitrary")),
        name="out_proj",
    )(x2, a_out, b_out, w_a, w_b)


FFN_HALO = HALO


def _ffn_up_kernel(x_ref, xp_ref, xn_ref, g_ref, wg_ref, wv_ref, cwg_ref, cwv_ref, cbg_ref, cbv_ref,
                   o_ref, h_ref, *, tiles_per_seq):
    tm = x_ref.shape[0]
    i = pl.program_id(0)

    def norm(v):
        var = jnp.mean(v * v, axis=-1, keepdims=True)
        return v * lax.rsqrt(var + NORM_EPS) * g_ref[...]

    @pl.when(pl.program_id(1) == 0)
    def _():
        first = (i % tiles_per_seq) == 0
        last = (i % tiles_per_seq) == tiles_per_seq - 1
        h_ref[0:FFN_HALO, :] = jnp.where(first, 0.0, norm(xp_ref[...])).astype(BF16)
        h_ref[FFN_HALO + tm:, :] = jnp.where(last, 0.0, norm(xn_ref[...])).astype(BF16)

        def body(r, carry):
            r0 = pl.multiple_of(r * 128, 128)
            h_ref[pl.ds(FFN_HALO + r0, 128), :] = norm(x_ref[pl.ds(r0, 128), :]).astype(BF16)
            return carry
        lax.fori_loop(0, tm // 128, body, 0)

    rows = tm + 2 * FFN_HALO

    def conv(w_ref, cw_ref, cb_ref):
        up = jnp.dot(h_ref[...], w_ref[...], preferred_element_type=F32)
        acc = up[FFN_HALO:FFN_HALO + tm, :] * cw_ref[1:2, :] + cb_ref[...]
        acc = acc + pltpu.roll(up, 1, axis=0)[FFN_HALO:FFN_HALO + tm, :] * cw_ref[0:1, :]
        acc = acc + pltpu.roll(up, rows - 1, axis=0)[FFN_HALO:FFN_HALO + tm, :] * cw_ref[2:3, :]
        return acc

    gate = conv(wg_ref, cwg_ref, cbg_ref)
    val = conv(wv_ref, cwv_ref, cbv_ref)
    o_ref[...] = (_silu(gate) * val).astype(o_ref.dtype)


def _ffn_up(x1, gain, w_up, conv_w, conv_b, *, seq, tm, tn):
    t, d = x1.shape
    f = w_up.shape[1] // 2
    nj = f // tn
    tiles_per_seq = seq // tm
    hb = tm // FFN_HALO
    nhb = t // FFN_HALO
    return pl.pallas_call(
        functools.partial(_ffn_up_kernel, tiles_per_seq=tiles_per_seq),
        grid=(t // tm, nj),
        in_specs=[
            pl.BlockSpec((tm, d), lambda i, j: (i, 0)),
            pl.BlockSpec((FFN_HALO, d), lambda i, j: (jnp.maximum(i * hb - 1, 0), 0)),
            pl.BlockSpec((FFN_HALO, d), lambda i, j: (jnp.minimum((i + 1) * hb, nhb - 1), 0)),
            pl.BlockSpec((1, d), lambda i, j: (0, 0)),
            pl.BlockSpec((d, tn), lambda i, j: (0, j)),
            pl.BlockSpec((d, tn), lambda i, j: (0, j + nj)),
            pl.BlockSpec((3, tn), lambda i, j: (0, j)),
            pl.BlockSpec((3, tn), lambda i, j: (0, j + nj)),
            pl.BlockSpec((1, tn), lambda i, j: (0, j)),
            pl.BlockSpec((1, tn), lambda i, j: (0, j + nj)),
        ],
        out_specs=pl.BlockSpec((tm, tn), lambda i, j: (i, j)),
        out_shape=jax.ShapeDtypeStruct((t, f), BF16),
        scratch_shapes=[pltpu.VMEM((tm + 2 * FFN_HALO, d), BF16)],
        compiler_params=_params(("parallel", "arbitrary")),
        name="ffn_up",
    )(x1, x1, x1, gain, w_up, w_up, conv_w, conv_w, conv_b, conv_b)


def _ffn_down_kernel(x_ref, a_ref, w_ref, g_ref, o_ref, acc_ref):
    k = pl.program_id(1)

    @pl.when(k == 0)
    def _():
        acc_ref[...] = x_ref[...]

    acc_ref[...] += jnp.dot(a_ref[...], w_ref[...], preferred_element_type=F32)

    @pl.when(k == pl.num_programs(1) - 1)
    def _():
        v = acc_ref[...]
        var = jnp.mean(v * v, axis=-1, keepdims=True)
        o_ref[...] = v * lax.rsqrt(var + NORM_EPS) * g_ref[...]


def _ffn_down(x1, act, w_down, gain, *, tm, tk):
    t, d = x1.shape
    f = act.shape[1]
    return pl.pallas_call(
        _ffn_down_kernel,
        grid=(t // tm, f // tk),
        in_specs=[
            pl.BlockSpec((tm, d), lambda i, k: (i, 0)),
            pl.BlockSpec((tm, tk), lambda i, k: (i, k)),
            pl.BlockSpec((tk, d), lambda i, k: (k, 0)),
            pl.BlockSpec((1, d), lambda i, k: (0, 0)),
        ],
        out_specs=pl.BlockSpec((tm, d), lambda i, k: (i, 0)),
        out_shape=jax.ShapeDtypeStruct((t, d), F32),
        scratch_shapes=[pltpu.VMEM((tm, d), F32)],
        compiler_params=_params(("parallel", "arbitrary")),
        name="ffn_down",
    )(x1, act, w_down, gain)


def _layer(x2, batch, seq, norm_mix_w, w_in, fourier_w, ssm_conv_w, ssm_conv_b, dt_bias_fwd, a_log_fwd,
           dt_bias_bwd, a_log_bwd, ssm_d, ssm_norm_w, w_out, norm_ffn_w, w_up, ffn_conv_w, ffn_conv_b,
           w_down, final_gain, tiles):
    d = x2.shape[1]
    g, r = SSM_GROUPS, HEADS_PER_GROUP
    fw = fourier_w.shape[0] * fourier_w.shape[1]
    sw = g * GROUP_X
    gn = g * SSM_STATE
    o_z, o_x = fw, fw + sw
    o_b, o_c, o_dt = o_x + sw, o_x + sw + gn, o_x + sw + 2 * gn

    def per_group(lo, wdt):
        return w_in[:, lo:lo + g * wdt].reshape(d, g, wdt)
    w_groups = jnp.concatenate([per_group(o_z, GROUP_X), per_group(o_x, GROUP_X),
                                per_group(o_b, SSM_STATE), per_group(o_c, SSM_STATE)], axis=2)
    w_main = jnp.concatenate([w_in[:, :fw], w_groups.reshape(d, g * GROUP_COLS)], axis=1).astype(BF16)
    wdt = w_in[:, o_dt:].T.reshape(g, r, d)
    wdt_t = jnp.pad(wdt, ((0, 0), (0, HEAD_ROWS - r), (0, 0))).reshape(g * HEAD_ROWS, d)

    proj, dt_t = _in_proj(x2, norm_mix_w[None, :], w_main, wdt_t, tm=tiles["in_tm"], tn=tiles["in_tn"])

    c = fourier_w.shape[1]
    cos_s, sin_s = _dft_tables(seq, 64)
    cos_c, sin_c = _dft_tables(c, 8)
    a_out = _fourier(cos_s.astype(BF16), sin_s.astype(BF16), proj, cos_c.astype(BF16), sin_c.astype(BF16),
                     fourier_w.astype(BF16), batch=batch, seq=seq, tm=tiles["f_tm"])

    def conv_group(v):
        lead = v.shape[:-1]
        parts = [v[..., :sw].reshape(*lead, g, GROUP_X), v[..., sw:sw + gn].reshape(*lead, g, SSM_STATE),
                 v[..., sw + gn:].reshape(*lead, g, SSM_STATE)]
        return jnp.moveaxis(jnp.concatenate(parts, axis=-1), -2, 0)
    conv_w = conv_group(ssm_conv_w)
    conv_b = conv_group(ssm_conv_b[None, :])
    hp = jnp.stack([dt_bias_fwd, a_log_fwd, dt_bias_bwd, a_log_bwd], axis=-1).reshape(g, r, 4)
    hp = jnp.pad(hp, ((0, 0), (0, HEAD_ROWS - r), (0, 0)))
    d_exp = jnp.repeat(ssm_d, SSM_HEAD_DIM).reshape(g, 1, GROUP_X)
    b_out = _ssd(proj, dt_t, conv_w, conv_b, hp, d_exp, ssm_norm_w.reshape(g, 1, GROUP_X),
                 batch=batch, seq=seq)

    w_out_b = w_out.astype(BF16)
    x1 = _out_proj(x2, a_out, b_out, w_out_b[:fw], w_out_b[fw:], tm=tiles["o_tm"], tn=tiles["o_tn"])

    act = _ffn_up(x1, norm_ffn_w[None, :], w_up.astype(BF16), ffn_conv_w, ffn_conv_b[None, :],
                  seq=seq, tm=tiles["u_tm"], tn=tiles["u_tn"])
    return _ffn_down(x1, act, w_down.astype(BF16), final_gain[None, :], tm=tiles["d_tm"], tk=tiles["d_tk"])


TILES = dict(in_tm=512, in_tn=1024, f_tm=512, o_tm=512, o_tn=1024, u_tm=1024, u_tn=512, d_tm=1024, d_tk=512)


def kernel(x, norm_mix_w, w_in, fourier_w, ssm_conv_w, ssm_conv_b, dt_bias_fwd, a_log_fwd, dt_bias_bwd,
           a_log_bwd, ssm_d, ssm_norm_w, w_out, norm_ffn_w, w_up, ffn_conv_w, ffn_conv_b, w_down,
           norm_final_w):
    batch, seq, d = x.shape
    assert norm_mix_w.shape[0] == 1, "one layer"
    out = _layer(x.reshape(batch * seq, d), batch, seq, norm_mix_w[0], w_in[0], fourier_w[0], ssm_conv_w[0],
                 ssm_conv_b[0], dt_bias_fwd[0], a_log_fwd[0], dt_bias_bwd[0], a_log_bwd[0], ssm_d[0],
                 ssm_norm_w[0], w_out[0], norm_ffn_w[0], w_up[0], ffn_conv_w[0], ffn_conv_b[0], w_down[0],
                 norm_final_w, TILES)
    return out.reshape(batch, seq, d)
```

```python
import functools

import jax
import jax.numpy as jnp
import numpy as np
from jax import lax
from jax.experimental import pallas as pl
from jax.experimental.pallas import tpu as pltpu

NORM_EPS = 1e-5
FOURIER_GROUPS = 8
SSM_GROUPS = 8
HEADS_PER_GROUP = 6
HEAD_ROWS = 8
SSM_HEAD_DIM = 64
SSM_STATE = 128
CHUNK = 128
GROUP_X = HEADS_PER_GROUP * SSM_HEAD_DIM
GROUP_XBC = GROUP_X + 2 * SSM_STATE
GROUP_COLS = GROUP_X + GROUP_XBC
HALO = 16
VMEM_LIMIT = 56 * 1024 * 1024

F32 = jnp.float32
BF16 = jnp.bfloat16


def _params(semantics):
    return pltpu.CompilerParams(dimension_semantics=semantics, vmem_limit_bytes=VMEM_LIMIT)


def _silu(v):
    return v / (1.0 + jnp.exp(-v))


def _in_proj_kernel(x_ref, g_ref, w_ref, wdt_ref, o_ref, dt_ref, h_ref, *, sub):
    @pl.when(pl.program_id(1) == 0)
    def _():
        def body(r, carry):
            r0 = pl.multiple_of(r * sub, sub)
            xs = x_ref[pl.ds(r0, sub), :]
            var = jnp.mean(xs * xs, axis=-1, keepdims=True)
            hs = xs * lax.rsqrt(var + NORM_EPS) * g_ref[...]
            h_ref[pl.ds(r0, sub), :] = hs.astype(BF16)
            return carry
        lax.fori_loop(0, x_ref.shape[0] // sub, body, 0)
        dt = jnp.dot(h_ref[...], wdt_ref[...], preferred_element_type=F32)
        lanes = wdt_ref.shape[1]
        for r in range(x_ref.shape[0] // lanes):
            dt_ref[:, r * lanes:(r + 1) * lanes] = jnp.transpose(
                dt[r * lanes:(r + 1) * lanes, :])[:dt_ref.shape[0], :]

    o_ref[...] = jnp.dot(h_ref[...], w_ref[...], preferred_element_type=F32).astype(o_ref.dtype)


def _in_proj(x2, gain, w, wdt, *, nh, tm, tn):
    t, d = x2.shape
    n = w.shape[1]
    return pl.pallas_call(
        functools.partial(_in_proj_kernel, sub=128),
        grid=(t // tm, n // tn),
        in_specs=[
            pl.BlockSpec((tm, d), lambda i, j: (i, 0)),
            pl.BlockSpec((1, d), lambda i, j: (0, 0)),
            pl.BlockSpec((d, tn), lambda i, j: (0, j)),
            pl.BlockSpec(wdt.shape, lambda i, j: (0, 0)),
        ],
        out_specs=[
            pl.BlockSpec((tm, tn), lambda i, j: (i, j)),
            pl.BlockSpec((nh, tm), lambda i, j: (0, i)),
        ],
        out_shape=[jax.ShapeDtypeStruct((t, n), BF16), jax.ShapeDtypeStruct((nh, t), F32)],
        scratch_shapes=[pltpu.VMEM((tm, d), BF16)],
        compiler_params=_params(("parallel", "arbitrary")),
        name="in_proj",
    )(x2, gain, w, wdt)


def _fourier_kernel(cos_ref, sin_ref, u_ref, cc_ref, sc_ref, w_ref, o_ref, *, scale):
    u = u_ref[...]
    uc = jnp.dot(cos_ref[...], u, preferred_element_type=F32)
    us = jnp.dot(sin_ref[...], u, preferred_element_type=F32)
    c = cc_ref.shape[0]
    for g in range(w_ref.shape[0]):
        ucg = uc[:, g * c:(g + 1) * c].astype(BF16)
        usg = us[:, g * c:(g + 1) * c].astype(BF16)
        re = (jnp.dot(ucg, cc_ref[...], preferred_element_type=F32)
              - jnp.dot(usg, sc_ref[...], preferred_element_type=F32)) * scale
        o_ref[:, g * c:(g + 1) * c] = jnp.dot(
            re.astype(BF16), w_ref[g], preferred_element_type=F32).astype(o_ref.dtype)


def _fourier(cos_m, sin_m, proj, cc, sc, w_mix, *, batch, seq, tm):
    g, c, _ = w_mix.shape
    width = g * c
    scale = float(1.0 / np.sqrt(seq * c))
    nt = seq // tm
    return pl.pallas_call(
        functools.partial(_fourier_kernel, scale=scale),
        grid=(batch, nt),
        in_specs=[
            pl.BlockSpec((tm, seq), lambda b, i: (i, 0)),
            pl.BlockSpec((tm, seq), lambda b, i: (i, 0)),
            pl.BlockSpec((seq, width), lambda b, i: (b, 0)),
            pl.BlockSpec((c, c), lambda b, i: (0, 0)),
            pl.BlockSpec((c, c), lambda b, i: (0, 0)),
            pl.BlockSpec((g, c, c), lambda b, i: (0, 0, 0)),
        ],
        out_specs=pl.BlockSpec((tm, width), lambda b, i: (b * nt + i, 0)),
        out_shape=jax.ShapeDtypeStruct((batch * seq, width), BF16),
        compiler_params=_params(("parallel", "arbitrary")),
        name="fourier",
    )(cos_m, sin_m, proj, cc, sc, w_mix)


def _dft_tables(n, split):
    hi = n // split
    s = jnp.arange(n, dtype=jnp.int32)[None, :]
    k1 = jnp.arange(hi, dtype=jnp.int32)[:, None]
    k0 = jnp.arange(split, dtype=jnp.int32)[:, None]
    a1 = ((k1 * s * split) % n).astype(F32) * (2.0 * np.pi / n)
    a0 = ((k0 * s) % n).astype(F32) * (2.0 * np.pi / n)
    c1, s1 = jnp.cos(a1)[:, None, :], jnp.sin(a1)[:, None, :]
    c0, s0 = jnp.cos(a0)[None, :, :], jnp.sin(a0)[None, :, :]
    cos_m = (c1 * c0 - s1 * s0).reshape(n, n)
    sin_m = (s1 * c0 + c1 * s0).reshape(n, n)
    return cos_m, sin_m


PIECES_P, PIECES_YS, PIECES_SW = 3, 2, 2
COL_P, COL_YS, COL_SW = 0, PIECES_P * HEAD_ROWS, (PIECES_P + PIECES_YS) * HEAD_ROWS
COL_DIR = (PIECES_P + PIECES_YS + PIECES_SW) * HEAD_ROWS
SEL_YS = HEADS_PER_GROUP * CHUNK
SEL_SW = SEL_YS + GROUP_X
SEL_COLS = SEL_SW + GROUP_X
ROW_Q, ROW_CD = 0, HEAD_ROWS
ROW_DIR = 2 * HEAD_ROWS
LOG2E = float(np.log2(np.e))


def _selection_matrices():
    sel = np.zeros((2, CHUNK, SEL_COLS), np.float32)
    for d in range(2):
        for r in range(HEADS_PER_GROUP):
            for piece in range(PIECES_P):
                sel[d, d * COL_DIR + COL_P + piece * HEAD_ROWS + r, r * CHUNK:(r + 1) * CHUNK] = 1.0
            for piece in range(PIECES_YS):
                q = d * COL_DIR + COL_YS + piece * HEAD_ROWS + r
                sel[d, q, SEL_YS + r * SSM_HEAD_DIM:SEL_YS + (r + 1) * SSM_HEAD_DIM] = 1.0
            for piece in range(PIECES_SW):
                q = d * COL_DIR + COL_SW + piece * HEAD_ROWS + r
                sel[d, q, SEL_SW + r * SSM_HEAD_DIM:SEL_SW + (r + 1) * SSM_HEAD_DIM] = 1.0
    return sel


def _softplus(v):
    return jnp.maximum(v, 0.0) + jnp.log1p(jnp.exp(-jnp.abs(v)))


def _bf16_pieces(v, n):
    out = []
    for _ in range(n - 1):
        hi = v.astype(BF16).astype(F32)
        out.append(hi)
        v = v - hi
    out.append(v)
    return out


def _ssd_kernel(in_ref, dt_ref, cw_ref, cb_ref, hp_ref, d_ref, nw_ref, sel_ref, o_ref,
                xc_ref, bt_ref, colq_ref, rowq_ref, y_ref, st_ref):
    seq = in_ref.shape[0]
    nchunks = seq // CHUNK
    li = lax.broadcasted_iota(jnp.int32, (CHUNK, CHUNK), 0)
    si = lax.broadcasted_iota(jnp.int32, (CHUNK, CHUNK), 1)
    upper = (li <= si)
    upper_f = jnp.where(upper, 1.0, 0.0).astype(F32)
    lane_lo_row = lax.broadcasted_iota(jnp.int32, (1, 128), 1) < SSM_HEAD_DIM

    hp = hp_ref[...]
    bias_f, a_f = hp[:, 0:1], -jnp.exp(hp[:, 1:2])
    bias_b, a_b = hp[:, 2:3], -jnp.exp(hp[:, 3:4])

    def prep(c, carry):
        r0 = pl.multiple_of(c * CHUNK, CHUNK)
        cur = in_ref[pl.ds(r0, CHUNK), GROUP_X:].astype(F32)
        p0 = pl.multiple_of(jnp.maximum(r0 - HALO, 0), HALO)
        n0 = pl.multiple_of(jnp.minimum(r0 + CHUNK, seq - HALO), HALO)
        prev = in_ref[pl.ds(p0, HALO), GROUP_X:].astype(F32)
        nxt = in_ref[pl.ds(n0, HALO), GROUP_X:].astype(F32)
        prev = jnp.where(c > 0, prev, 0.0)
        nxt = jnp.where(c < nchunks - 1, nxt, 0.0)
        win = jnp.concatenate([prev, cur, nxt], axis=0)
        rows = CHUNK + 2 * HALO
        width = cw_ref.shape[0]
        acc = jnp.zeros((CHUNK, GROUP_XBC), F32) + cb_ref[...]
        for k in range(width):
            shift = (width // 2 - k) % rows
            tap = win if shift == 0 else pltpu.roll(win, shift, axis=0)
            acc = acc + tap[HALO:HALO + CHUNK, :] * cw_ref[k:k + 1, :]
        act = _silu(acc)
        xc_ref[pl.ds(r0, CHUNK), :] = act.astype(BF16)
        bt_ref[:, pl.ds(r0, CHUNK)] = jnp.transpose(act[:, GROUP_X:GROUP_X + SSM_STATE]).astype(BF16)

        dtr = dt_ref[:, pl.ds(r0, CHUNK)]
        dt_f = _softplus(dtr + bias_f)
        dt_b = _softplus(dtr + bias_b)
        adt_f = dt_f * a_f
        adt_b = dt_b * a_b
        cum_f = jnp.dot(adt_f, upper_f, precision=lax.Precision.HIGHEST, preferred_element_type=F32)
        cum_b = jnp.dot(adt_b, upper_f, precision=lax.Precision.HIGHEST, preferred_element_type=F32)
        tot_f = cum_f[:, CHUNK - 1:CHUNK]
        tot_b = cum_b[:, CHUNK - 1:CHUNK]
        ex_b = cum_b - adt_b
        p2_f = cum_f * LOG2E
        p2_b = -ex_b * LOG2E
        rows = (_bf16_pieces(p2_f, PIECES_P) + _bf16_pieces(jnp.exp(cum_f), PIECES_YS)
                + _bf16_pieces(jnp.exp(tot_f - cum_f) * dt_f, PIECES_SW)
                + _bf16_pieces(p2_b, PIECES_P) + _bf16_pieces(jnp.exp(tot_b - ex_b), PIECES_YS)
                + _bf16_pieces(jnp.exp(ex_b) * dt_b, PIECES_SW))
        rows.append(jnp.zeros((CHUNK - 2 * COL_DIR, CHUNK), F32))
        colq_ref[pl.ds(r0, CHUNK), :] = jnp.transpose(jnp.concatenate(rows, axis=0)).astype(BF16)
        rowq_ref[:, pl.ds(r0, CHUNK)] = jnp.concatenate([
            p2_f - jnp.log(dt_f) * LOG2E, jnp.broadcast_to(jnp.exp(tot_f), (HEAD_ROWS, CHUNK)),
            p2_b - jnp.log(dt_b) * LOG2E, jnp.broadcast_to(jnp.exp(tot_b), (HEAD_ROWS, CHUNK))], axis=0)
        return carry

    lax.fori_loop(0, nchunks, prep, 0)

    def chunk(c, direction):
        r0 = pl.multiple_of(c * CHUNK, CHUNK)
        xcb = xc_ref[pl.ds(r0, CHUNK), :]
        xs = xcb[:, :GROUP_X]
        bm_t = bt_ref[:, pl.ds(r0, CHUNK)]
        cm = xcb[:, GROUP_X + SSM_STATE:]
        rowq = rowq_ref[pl.ds(direction * ROW_DIR, ROW_DIR), pl.ds(r0, CHUNK)]
        bc = jnp.dot(colq_ref[pl.ds(r0, CHUNK), :], sel_ref[direction], preferred_element_type=F32)
        cb = jnp.dot(cm, bm_t, preferred_element_type=F32)
        mask = (li >= si) if direction == 0 else (si >= li)

        state = st_ref[...]
        y = jnp.dot(cm, state.astype(BF16), preferred_element_type=F32)
        y = y * bc[:, SEL_YS:SEL_SW]

        xs_lo = jnp.where(lane_lo_row, 1.0, 0.0).astype(BF16)
        tiles = []
        for j in range(HEADS_PER_GROUP // 2):
            xp = xs[:, 128 * j:128 * (j + 1)]
            acc = None
            for half, xm in ((0, xp * xs_lo), (1, xp * (1.0 - xs_lo).astype(BF16))):
                r = 2 * j + half
                qrow = rowq[ROW_Q + r:ROW_Q + r + 1, :]
                seg = bc[:, r * CHUNK:(r + 1) * CHUNK] - qrow
                m = (cb * jnp.exp2(jnp.where(mask, seg, -jnp.inf))).astype(BF16)
                d = jnp.dot(m, xm, preferred_element_type=F32)
                acc = d if acc is None else acc + d
            tiles.append(acc)
        y = y + jnp.concatenate(tiles, axis=1)

        xw = (xs.astype(F32) * bc[:, SEL_SW:]).astype(BF16)
        new = jnp.dot(bm_t, xw, preferred_element_type=F32)
        cd_tiles = []
        for j in range(HEADS_PER_GROUP // 2):
            a = rowq[ROW_CD + 2 * j:ROW_CD + 2 * j + 1, :]
            b = rowq[ROW_CD + 2 * j + 1:ROW_CD + 2 * j + 2, :]
            cd_tiles.append(jnp.where(lane_lo_row, a, b))
        st_ref[...] = state * jnp.concatenate(cd_tiles, axis=1) + new
        return r0, xs, y

    st_ref[...] = jnp.zeros_like(st_ref)

    def fwd(c, carry):
        r0, _, y = chunk(c, 0)
        y_ref[pl.ds(r0, CHUNK), :] = y
        return carry

    lax.fori_loop(0, nchunks, fwd, 0)

    st_ref[...] = jnp.zeros_like(st_ref)

    def bwd(i, carry):
        r0, xs, y = chunk(nchunks - 1 - i, 1)
        y = y + y_ref[pl.ds(r0, CHUNK), :] + d_ref[...] * xs.astype(F32)
        y = y * _silu(in_ref[pl.ds(r0, CHUNK), :GROUP_X].astype(F32))
        ms = jnp.mean(y * y, axis=-1, keepdims=True)
        o_ref[pl.ds(r0, CHUNK), :] = (y * lax.rsqrt(ms + NORM_EPS) * nw_ref[...]).astype(o_ref.dtype)
        return carry

    lax.fori_loop(0, nchunks, bwd, 0)


def _ssd(proj, dt_t, conv_w, conv_b, head_params, d_exp, norm_w, *, batch, seq):
    g = SSM_GROUPS
    width = conv_w.shape[1]
    return pl.pallas_call(
        _ssd_kernel,
        grid=(batch, g),
        in_specs=[
            pl.BlockSpec((seq, GROUP_COLS), lambda b, k: (b, k + 1)),
            pl.BlockSpec((HEAD_ROWS, seq), lambda b, k: (k, b)),
            pl.BlockSpec((None, width, GROUP_XBC), lambda b, k: (k, 0, 0)),
            pl.BlockSpec((None, 1, GROUP_XBC), lambda b, k: (k, 0, 0)),
            pl.BlockSpec((None, HEAD_ROWS, 4), lambda b, k: (k, 0, 0)),
            pl.BlockSpec((None, 1, GROUP_X), lambda b, k: (k, 0, 0)),
            pl.BlockSpec((None, 1, GROUP_X), lambda b, k: (k, 0, 0)),
            pl.BlockSpec((2, CHUNK, SEL_COLS), lambda b, k: (0, 0, 0)),
        ],
        out_specs=pl.BlockSpec((seq, GROUP_X), lambda b, k: (b, k)),
        out_shape=jax.ShapeDtypeStruct((batch * seq, g * GROUP_X), BF16),
        scratch_shapes=[
            pltpu.VMEM((seq, GROUP_XBC), BF16),
            pltpu.VMEM((SSM_STATE, seq), BF16),
            pltpu.VMEM((seq, CHUNK), BF16),
            pltpu.VMEM((2 * ROW_DIR, seq), F32),
            pltpu.VMEM((seq, GROUP_X), F32),
            pltpu.VMEM((SSM_STATE, GROUP_X), F32),
        ],
        compiler_params=_params(("parallel", "arbitrary")),
        name="ssd",
    )(proj, dt_t, conv_w, conv_b, head_params, d_exp, norm_w, jnp.asarray(_selection_matrices(), BF16))


def _out_proj_kernel(x_ref, a_ref, b_ref, wa_ref, wb_ref, o_ref):
    acc = jnp.dot(a_ref[...], wa_ref[...], preferred_element_type=F32)
    acc = acc + jnp.dot(b_ref[...], wb_ref[...], preferred_element_type=F32)
    o_ref[...] = x_ref[...] + acc


def _out_proj(x2, a_out, b_out, w_a, w_b, *, tm, tn):
    t, d = x2.shape
    ka, kb = a_out.shape[1], b_out.shape[1]
    return pl.pallas_call(
        _out_proj_kernel,
        grid=(t // tm, d // tn),
        in_specs=[
            pl.BlockSpec((tm, tn), lambda i, j: (i, j)),
            pl.BlockSpec((tm, ka), lambda i, j: (i, 0)),
            pl.BlockSpec((tm, kb), lambda i, j: (i, 0)),
            pl.BlockSpec((ka, tn), lambda i, j: (0, j)),
            pl.BlockSpec((kb, tn), lambda i, j: (0, j)),
        ],
        out_specs=pl.BlockSpec((tm, tn), lambda i, j: (i, j)),
        out_shape=jax.ShapeDtypeStruct((t, d), F32),
        compiler_params=_params(("parallel", "arbitrary")),
        name="out_proj",
    )(x2, a_out, b_out, w_a, w_b)


FFN_HALO = HALO


def _ffn_up_kernel(x_ref, xp_ref, xn_ref, g_ref, wg_ref, wv_ref, cwg_ref, cwv_ref, cbg_ref, cbv_ref,
                   o_ref, h_ref, *, tiles_per_seq):
    tm = x_ref.shape[0]
    i = pl.program_id(0)

    def norm(v):
        var = jnp.mean(v * v, axis=-1, keepdims=True)
        return v * lax.rsqrt(var + NORM_EPS) * g_ref[...]

    @pl.when(pl.program_id(1) == 0)
    def _():
        first = (i % tiles_per_seq) == 0
        last = (i % tiles_per_seq) == tiles_per_seq - 1
        h_ref[0:FFN_HALO, :] = jnp.where(first, 0.0, norm(xp_ref[...])).astype(BF16)
        h_ref[FFN_HALO + tm:, :] = jnp.where(last, 0.0, norm(xn_ref[...])).astype(BF16)

        def body(r, carry):
            r0 = pl.multiple_of(r * 128, 128)
            h_ref[pl.ds(FFN_HALO + r0, 128), :] = norm(x_ref[pl.ds(r0, 128), :]).astype(BF16)
            return carry
        lax.fori_loop(0, tm // 128, body, 0)

    rows = tm + 2 * FFN_HALO

    def conv(w_ref, cw_ref, cb_ref):
        up = jnp.dot(h_ref[...], w_ref[...], preferred_element_type=F32)
        acc = up[FFN_HALO:FFN_HALO + tm, :] * cw_ref[1:2, :] + cb_ref[...]
        acc = acc + pltpu.roll(up, 1, axis=0)[FFN_HALO:FFN_HALO + tm, :] * cw_ref[0:1, :]
        acc = acc + pltpu.roll(up, rows - 1, axis=0)[FFN_HALO:FFN_HALO + tm, :] * cw_ref[2:3, :]
        return acc

    gate = conv(wg_ref, cwg_ref, cbg_ref)
    val = conv(wv_ref, cwv_ref, cbv_ref)
    o_ref[...] = (_silu(gate) * val).astype(o_ref.dtype)


def _ffn_up(x1, gain, w_up, conv_w, conv_b, *, seq, tm, tn):
    t, d = x1.shape
    f = w_up.shape[1] // 2
    nj = f // tn
    tiles_per_seq = seq // tm
    hb = tm // FFN_HALO
    nhb = t // FFN_HALO
    return pl.pallas_call(
        functools.partial(_ffn_up_kernel, tiles_per_seq=tiles_per_seq),
        grid=(t // tm, nj),
        in_specs=[
            pl.BlockSpec((tm, d), lambda i, j: (i, 0)),
            pl.BlockSpec((FFN_HALO, d), lambda i, j: (jnp.maximum(i * hb - 1, 0), 0)),
            pl.BlockSpec((FFN_HALO, d), lambda i, j: (jnp.minimum((i + 1) * hb, nhb - 1), 0)),
            pl.BlockSpec((1, d), lambda i, j: (0, 0)),
            pl.BlockSpec((d, tn), lambda i, j: (0, j)),
            pl.BlockSpec((d, tn), lambda i, j: (0, j + nj)),
            pl.BlockSpec((3, tn), lambda i, j: (0, j)),
            pl.BlockSpec((3, tn), lambda i, j: (0, j + nj)),
            pl.BlockSpec((1, tn), lambda i, j: (0, j)),
            pl.BlockSpec((1, tn), lambda i, j: (0, j + nj)),
        ],
        out_specs=pl.BlockSpec((tm, tn), lambda i, j: (i, j)),
        out_shape=jax.ShapeDtypeStruct((t, f), BF16),
        scratch_shapes=[pltpu.VMEM((tm + 2 * FFN_HALO, d), BF16)],
        compiler_params=_params(("parallel", "arbitrary")),
        name="ffn_up",
    )(x1, x1, x1, gain, w_up, w_up, conv_w, conv_w, conv_b, conv_b)


def _ffn_down_kernel(x_ref, a_ref, w_ref, g_ref, o_ref, acc_ref):
    k = pl.program_id(1)

    @pl.when(k == 0)
    def _():
        acc_ref[...] = x_ref[...]

    acc_ref[...] += jnp.dot(a_ref[...], w_ref[...], preferred_element_type=F32)

    @pl.when(k == pl.num_programs(1) - 1)
    def _():
        v = acc_ref[...]
        var = jnp.mean(v * v, axis=-1, keepdims=True)
        o_ref[...] = v * lax.rsqrt(var + NORM_EPS) * g_ref[...]


def _ffn_down(x1, act, w_down, gain, *, tm, tk):
    t, d = x1.shape
    f = act.shape[1]
    return pl.pallas_call(
        _ffn_down_kernel,
        grid=(t // tm, f // tk),
        in_specs=[
            pl.BlockSpec((tm, d), lambda i, k: (i, 0)),
            pl.BlockSpec((tm, tk), lambda i, k: (i, k)),
            pl.BlockSpec((tk, d), lambda i, k: (k, 0)),
            pl.BlockSpec((1, d), lambda i, k: (0, 0)),
        ],
        out_specs=pl.BlockSpec((tm, d), lambda i, k: (i, 0)),
        out_shape=jax.ShapeDtypeStruct((t, d), F32),
        scratch_shapes=[pltpu.VMEM((tm, d), F32)],
        compiler_params=_params(("parallel", "arbitrary")),
        name="ffn_down",
    )(x1, act, w_down, gain)


def _layer(x2, batch, seq, norm_mix_w, w_in, fourier_w, ssm_conv_w, ssm_conv_b, dt_bias_fwd, a_log_fwd,
           dt_bias_bwd, a_log_bwd, ssm_d, ssm_norm_w, w_out, norm_ffn_w, w_up, ffn_conv_w, ffn_conv_b,
           w_down, final_gain, tiles):
    d = x2.shape[1]
    g, r = SSM_GROUPS, HEADS_PER_GROUP
    fw = fourier_w.shape[0] * fourier_w.shape[1]
    sw = g * GROUP_X
    gn = g * SSM_STATE
    o_z, o_x = fw, fw + sw
    o_b, o_c, o_dt = o_x + sw, o_x + sw + gn, o_x + sw + 2 * gn

    def per_group(lo, wdt):
        return w_in[:, lo:lo + g * wdt].reshape(d, g, wdt)
    w_groups = jnp.concatenate([per_group(o_z, GROUP_X), per_group(o_x, GROUP_X),
                                per_group(o_b, SSM_STATE), per_group(o_c, SSM_STATE)], axis=2)
    w_main = jnp.concatenate([w_in[:, :fw], w_groups.reshape(d, g * GROUP_COLS)], axis=1).astype(BF16)
    wdt = jnp.pad(w_in[:, o_dt:].reshape(d, g, r), ((0, 0), (0, 0), (0, HEAD_ROWS - r)))
    wdt = jnp.pad(wdt.reshape(d, g * HEAD_ROWS), ((0, 0), (0, 128 - g * HEAD_ROWS))).astype(BF16)

    proj, dt_t = _in_proj(x2, norm_mix_w[None, :], w_main, wdt, nh=g * HEAD_ROWS,
                          tm=tiles["in_tm"], tn=tiles["in_tn"])

    c = fourier_w.shape[1]
    cos_s, sin_s = _dft_tables(seq, 64)
    cos_c, sin_c = _dft_tables(c, 8)
    a_out = _fourier(cos_s.astype(BF16), sin_s.astype(BF16), proj, cos_c.astype(BF16), sin_c.astype(BF16),
                     fourier_w.astype(BF16), batch=batch, seq=seq, tm=tiles["f_tm"])

    def conv_group(v):
        lead = v.shape[:-1]
        parts = [v[..., :sw].reshape(*lead, g, GROUP_X), v[..., sw:sw + gn].reshape(*lead, g, SSM_STATE),
                 v[..., sw + gn:].reshape(*lead, g, SSM_STATE)]
        return jnp.moveaxis(jnp.concatenate(parts, axis=-1), -2, 0)
    conv_w = conv_group(ssm_conv_w)
    conv_b = conv_group(ssm_conv_b[None, :])
    hp = jnp.stack([dt_bias_fwd, a_log_fwd, dt_bias_bwd, a_log_bwd], axis=-1).reshape(g, r, 4)
    hp = jnp.pad(hp, ((0, 0), (0, HEAD_ROWS - r), (0, 0)))
    d_exp = jnp.repeat(ssm_d, SSM_HEAD_DIM).reshape(g, 1, GROUP_X)
    b_out = _ssd(proj, dt_t, conv_w, conv_b, hp, d_exp, ssm_norm_w.reshape(g, 1, GROUP_X),
                 batch=batch, seq=seq)

    w_out_b = w_out.astype(BF16)
    x1 = _out_proj(x2, a_out, b_out, w_out_b[:fw], w_out_b[fw:], tm=tiles["o_tm"], tn=tiles["o_tn"])

    act = _ffn_up(x1, norm_ffn_w[None, :], w_up.astype(BF16), ffn_conv_w, ffn_conv_b[None, :],
                  seq=seq, tm=tiles["u_tm"], tn=tiles["u_tn"])
    return _ffn_down(x1, act, w_down.astype(BF16), final_gain[None, :], tm=tiles["d_tm"], tk=tiles["d_tk"])


TILES = dict(in_tm=1024, in_tn=1024, f_tm=512, o_tm=1024, o_tn=512, u_tm=1024, u_tn=512, d_tm=512, d_tk=1408)


def kernel(x, norm_mix_w, w_in, fourier_w, ssm_conv_w, ssm_conv_b, dt_bias_fwd, a_log_fwd, dt_bias_bwd,
           a_log_bwd, ssm_d, ssm_norm_w, w_out, norm_ffn_w, w_up, ffn_conv_w, ffn_conv_b, w_down,
           norm_final_w):
    batch, seq, d = x.shape
    assert norm_mix_w.shape[0] == 1, "one layer"
    out = _layer(x.reshape(batch * seq, d), batch, seq, norm_mix_w[0], w_in[0], fourier_w[0], ssm_conv_w[0],
                 ssm_conv_b[0], dt_bias_fwd[0], a_log_fwd[0], dt_bias_bwd[0], a_log_bwd[0], ssm_d[0],
                 ssm_norm_w[0], w_out[0], norm_ffn_w[0], w_up[0], ffn_conv_w[0], ffn_conv_b[0], w_down[0],
                 norm_final_w, TILES)
    return out.reshape(batch, seq, d)
```

```python
import functools

import jax
import jax.numpy as jnp
import numpy as np
from jax import lax
from jax.experimental import pallas as pl
from jax.experimental.pallas import tpu as pltpu

NORM_EPS = 1e-5
FOURIER_GROUPS = 8
SSM_GROUPS = 8
HEADS_PER_GROUP = 6
HEAD_ROWS = 8
SSM_HEAD_DIM = 64
SSM_STATE = 128
CHUNK = 128
GROUP_X = HEADS_PER_GROUP * SSM_HEAD_DIM
GROUP_XBC = GROUP_X + 2 * SSM_STATE
GROUP_COLS = GROUP_X + GROUP_XBC
HALO = 16
VMEM_LIMIT = 56 * 1024 * 1024

F32 = jnp.float32
BF16 = jnp.bfloat16


def _params(semantics):
    return pltpu.CompilerParams(dimension_semantics=semantics, vmem_limit_bytes=VMEM_LIMIT)


def _silu(v):
    return v / (1.0 + jnp.exp(-v))


def _in_proj_kernel(x_ref, g_ref, w_ref, wdt_ref, o_ref, dt_ref, h_ref, *, sub):
    @pl.when(pl.program_id(1) == 0)
    def _():
        def body(r, carry):
            r0 = pl.multiple_of(r * sub, sub)
            xs = x_ref[pl.ds(r0, sub), :]
            var = jnp.mean(xs * xs, axis=-1, keepdims=True)
            hs = xs * lax.rsqrt(var + NORM_EPS) * g_ref[...]
            h_ref[pl.ds(r0, sub), :] = hs.astype(BF16)
            return carry
        lax.fori_loop(0, x_ref.shape[0] // sub, body, 0)
        dt = jnp.dot(h_ref[...], wdt_ref[...], preferred_element_type=F32)
        lanes = wdt_ref.shape[1]
        for r in range(x_ref.shape[0] // lanes):
            dt_ref[:, r * lanes:(r + 1) * lanes] = jnp.transpose(
                dt[r * lanes:(r + 1) * lanes, :])[:dt_ref.shape[0], :]

    o_ref[...] = jnp.dot(h_ref[...], w_ref[...], preferred_element_type=F32).astype(o_ref.dtype)


def _in_proj(x2, gain, w, wdt, *, nh, tm, tn):
    t, d = x2.shape
    n = w.shape[1]
    return pl.pallas_call(
        functools.partial(_in_proj_kernel, sub=128),
        grid=(t // tm, n // tn),
        in_specs=[
            pl.BlockSpec((tm, d), lambda i, j: (i, 0)),
            pl.BlockSpec((1, d), lambda i, j: (0, 0)),
            pl.BlockSpec((d, tn), lambda i, j: (0, j)),
            pl.BlockSpec(wdt.shape, lambda i, j: (0, 0)),
        ],
        out_specs=[
            pl.BlockSpec((tm, tn), lambda i, j: (i, j)),
            pl.BlockSpec((nh, tm), lambda i, j: (0, i)),
        ],
        out_shape=[jax.ShapeDtypeStruct((t, n), BF16), jax.ShapeDtypeStruct((nh, t), F32)],
        scratch_shapes=[pltpu.VMEM((tm, d), BF16)],
        compiler_params=_params(("parallel", "arbitrary")),
        name="in_proj",
    )(x2, gain, w, wdt)


def _fourier_kernel(cos_ref, sin_ref, u_ref, cc_ref, sc_ref, w_ref, o_ref, *, scale):
    u = u_ref[...]
    uc = jnp.dot(cos_ref[...], u, preferred_element_type=F32)
    us = jnp.dot(sin_ref[...], u, preferred_element_type=F32)
    c = cc_ref.shape[0]
    for g in range(w_ref.shape[0]):
        ucg = uc[:, g * c:(g + 1) * c].astype(BF16)
        usg = us[:, g * c:(g + 1) * c].astype(BF16)
        re = (jnp.dot(ucg, cc_ref[...], preferred_element_type=F32)
              - jnp.dot(usg, sc_ref[...], preferred_element_type=F32)) * scale
        o_ref[:, g * c:(g + 1) * c] = jnp.dot(
            re.astype(BF16), w_ref[g], preferred_element_type=F32).astype(o_ref.dtype)


def _fourier(cos_m, sin_m, proj, cc, sc, w_mix, *, batch, seq, tm):
    g, c, _ = w_mix.shape
    width = g * c
    scale = float(1.0 / np.sqrt(seq * c))
    nt = seq // tm
    return pl.pallas_call(
        functools.partial(_fourier_kernel, scale=scale),
        grid=(batch, nt),
        in_specs=[
            pl.BlockSpec((tm, seq), lambda b, i: (i, 0)),
            pl.BlockSpec((tm, seq), lambda b, i: (i, 0)),
            pl.BlockSpec((seq, width), lambda b, i: (b, 0)),
            pl.BlockSpec((c, c), lambda b, i: (0, 0)),
            pl.BlockSpec((c, c), lambda b, i: (0, 0)),
            pl.BlockSpec((g, c, c), lambda b, i: (0, 0, 0)),
        ],
        out_specs=pl.BlockSpec((tm, width), lambda b, i: (b * nt + i, 0)),
        out_shape=jax.ShapeDtypeStruct((batch * seq, width), BF16),
        compiler_params=_params(("parallel", "arbitrary")),
        name="fourier",
    )(cos_m, sin_m, proj, cc, sc, w_mix)


def _dft_tables(n, split):
    hi = n // split
    s = jnp.arange(n, dtype=jnp.int32)[None, :]
    k1 = jnp.arange(hi, dtype=jnp.int32)[:, None]
    k0 = jnp.arange(split, dtype=jnp.int32)[:, None]
    a1 = ((k1 * s * split) % n).astype(F32) * (2.0 * np.pi / n)
    a0 = ((k0 * s) % n).astype(F32) * (2.0 * np.pi / n)
    c1, s1 = jnp.cos(a1)[:, None, :], jnp.sin(a1)[:, None, :]
    c0, s0 = jnp.cos(a0)[None, :, :], jnp.sin(a0)[None, :, :]
    cos_m = (c1 * c0 - s1 * s0).reshape(n, n)
    sin_m = (s1 * c0 + c1 * s0).reshape(n, n)
    return cos_m, sin_m


PIECES_P = 3
COL_DIR = PIECES_P * HEAD_ROWS
SEL_COLS = HEADS_PER_GROUP * CHUNK
ROW_Q, ROW_YS, ROW_W, ROW_CD = 0, 1, 2, 3
ROW_TABLES = 4
LOG2E = float(np.log2(np.e))
CONV_TAPS = 5
WIN_ROWS = CHUNK + 2 * HALO


def _selection_matrices():
    sel = np.zeros((2, CHUNK, SEL_COLS), np.float32)
    for d in range(2):
        for r in range(HEADS_PER_GROUP):
            for piece in range(PIECES_P):
                sel[d, d * COL_DIR + piece * HEAD_ROWS + r, r * CHUNK:(r + 1) * CHUNK] = 1.0
    return sel


def _shift_matrix():
    taps = [k for k in range(CONV_TAPS) if k != CONV_TAPS // 2]
    m = np.zeros((len(taps) * CHUNK, WIN_ROWS), np.float32)
    for i, k in enumerate(taps):
        for t in range(CHUNK):
            m[i * CHUNK + t, HALO + t + k - CONV_TAPS // 2] = 1.0
    return m


def _softplus(v):
    return jnp.maximum(v, 0.0) + jnp.log1p(jnp.exp(-jnp.abs(v)))


def _bf16_pieces(v, n):
    out = []
    for _ in range(n - 1):
        hi = v.astype(BF16).astype(F32)
        out.append(hi)
        v = v - hi
    out.append(v)
    return out


def _ssd_kernel(in_ref, dt_ref, cw_ref, cb_ref, hp_ref, d_ref, nw_ref, sel_ref, shift_ref, o_ref,
                xc_ref, xt_ref, colq_ref, rowq_ref, y_ref, st_ref, pb_ref, cbm_ref):
    seq = in_ref.shape[0]
    nchunks = seq // CHUNK
    half = nchunks // 2
    li = lax.broadcasted_iota(jnp.int32, (CHUNK, CHUNK), 0)
    si = lax.broadcasted_iota(jnp.int32, (CHUNK, CHUNK), 1)
    upper_b = jnp.where(li <= si, 1.0, 0.0).astype(BF16)
    lane_lo = si < SSM_HEAD_DIM
    lane_lo_row = lax.broadcasted_iota(jnp.int32, (1, 128), 1) < SSM_HEAD_DIM

    hp = hp_ref[...]
    bias_f, a_f = hp[:, 0:1], -jnp.exp(hp[:, 1:2])
    bias_b, a_b = hp[:, 2:3], -jnp.exp(hp[:, 3:4])

    def prep(c):
        r0 = pl.multiple_of(c * CHUNK, CHUNK)
        cur = in_ref[pl.ds(r0, CHUNK), GROUP_X:]
        p0 = pl.multiple_of(jnp.maximum(r0 - HALO, 0), HALO)
        n0 = pl.multiple_of(jnp.minimum(r0 + CHUNK, seq - HALO), HALO)
        prev = in_ref[pl.ds(p0, HALO), GROUP_X:]
        nxt = in_ref[pl.ds(n0, HALO), GROUP_X:]
        prev = jnp.where(c > 0, prev, jnp.zeros_like(prev))
        nxt = jnp.where(c < nchunks - 1, nxt, jnp.zeros_like(nxt))
        win = jnp.concatenate([prev, cur, nxt], axis=0)
        taps = jnp.dot(shift_ref[...], win, preferred_element_type=F32)
        mid = CONV_TAPS // 2
        acc = cur.astype(F32) * cw_ref[mid:mid + 1, :] + cb_ref[...]
        for i, k in enumerate([k for k in range(CONV_TAPS) if k != mid]):
            acc = acc + taps[i * CHUNK:(i + 1) * CHUNK, :] * cw_ref[k:k + 1, :]
        act = _silu(acc)
        xc_ref[pl.ds(r0, CHUNK), :] = act.astype(BF16)
        xt_ref[:, pl.ds(r0, CHUNK)] = jnp.concatenate(
            [jnp.transpose(act[:, 128 * j:128 * (j + 1)]) for j in range(GROUP_X // 128)], axis=0).astype(BF16)

    def prep_pair(i, carry):
        prep(2 * i)
        prep(2 * i + 1)
        return carry

    lax.fori_loop(0, half, prep_pair, 0)

    trows = nchunks * HEAD_ROWS
    dt_all = dt_ref[...]
    dtr = jnp.concatenate([dt_all[:, c * CHUNK:(c + 1) * CHUNK] for c in range(nchunks)], axis=0)
    per_row = lambda col: jnp.concatenate([jnp.broadcast_to(col, (HEAD_ROWS, CHUNK))] * nchunks, axis=0)
    lane_const = lambda v: jnp.broadcast_to(v, (trows, CHUNK))

    def cumsum(v):
        stacked = jnp.concatenate(_bf16_pieces(v, 3), axis=0).astype(BF16)
        out = jnp.dot(stacked, upper_b, preferred_element_type=F32)
        return (out[:trows] + out[trows:2 * trows]) + out[2 * trows:]

    dt_f = _softplus(dtr + per_row(bias_f))
    dt_b = _softplus(dtr + per_row(bias_b))
    adt_f = dt_f * per_row(a_f)
    adt_b = dt_b * per_row(a_b)
    cum_f = cumsum(adt_f)
    cum_b = cumsum(adt_b)
    tot_f = cum_f[:, CHUNK - 1:CHUNK]
    tot_b = cum_b[:, CHUNK - 1:CHUNK]
    ex_b = cum_b - adt_b
    p2_f = cum_f * LOG2E
    p2_b = -ex_b * LOG2E
    tables = [p2_f - jnp.log(dt_f) * LOG2E, jnp.zeros((trows, CHUNK), F32),
              jnp.exp(tot_f - cum_f) * dt_f, lane_const(jnp.exp(tot_f)),
              p2_b - jnp.log(dt_b) * LOG2E, lane_const(tot_b * LOG2E),
              jnp.exp(ex_b) * dt_b, lane_const(jnp.exp(tot_b))]
    for q, tab in enumerate(tables):
        rowq_ref[q] = tab
    pieces = _bf16_pieces(p2_f, PIECES_P) + _bf16_pieces(p2_b, PIECES_P)
    pad = jnp.zeros((CHUNK - 2 * COL_DIR, CHUNK), F32)
    for c in range(nchunks):
        rows = [p[c * HEAD_ROWS:(c + 1) * HEAD_ROWS, :] for p in pieces] + [pad]
        colq_ref[c * CHUNK:(c + 1) * CHUNK, :] = jnp.transpose(jnp.concatenate(rows, axis=0)).astype(BF16)

    nt = (((1,), (1,)), ((), ()))

    def stage_ahead(c, direction):
        r0 = pl.multiple_of(c * CHUNK, CHUNK)
        pb_ref[direction] = jnp.dot(colq_ref[pl.ds(r0, CHUNK), :], sel_ref[direction],
                                    preferred_element_type=F32)
        cbm_ref[direction] = lax.dot_general(
            xc_ref[pl.ds(r0, CHUNK), GROUP_X + SSM_STATE:], xc_ref[pl.ds(r0, CHUNK), GROUP_X:GROUP_X + SSM_STATE],
            nt, preferred_element_type=F32)

    def chunk(c, direction):
        r0 = pl.multiple_of(c * CHUNK, CHUNK)
        xcb = xc_ref[pl.ds(r0, CHUNK), :]
        xs = xcb[:, :GROUP_X]
        bm = xcb[:, GROUP_X:GROUP_X + SSM_STATE]
        cm = xcb[:, GROUP_X + SSM_STATE:]
        r8 = pl.multiple_of(c * HEAD_ROWS, HEAD_ROWS)
        tabs = [rowq_ref[direction * ROW_TABLES + q, pl.ds(r8, HEAD_ROWS), :] for q in range(ROW_TABLES)]
        row = lambda q, r: tabs[q][r:r + 1, :]
        pb = pb_ref[direction]
        cb = cbm_ref[direction]
        step = 1 if direction == 0 else -1
        stage_ahead(jnp.clip(c + step, 0, nchunks - 1), direction)
        mask = (li >= si) if direction == 0 else (si >= li)

        state_t = st_ref[direction]
        y_off = lax.dot_general(cm, state_t.astype(BF16), nt, preferred_element_type=F32)

        xs_lo = jnp.where(lane_lo_row, 1.0, 0.0).astype(BF16)
        xs_hi = jnp.where(lane_lo_row, 0.0, 1.0).astype(BF16)
        tiles = []
        for j in range(HEADS_PER_GROUP // 2):
            xp = xs[:, 128 * j:128 * (j + 1)]
            pa = pb[:, 2 * j * CHUNK:(2 * j + 1) * CHUNK]
            pc = pb[:, (2 * j + 1) * CHUNK:(2 * j + 2) * CHUNK]
            ys = jnp.exp2(jnp.where(lane_lo, pa, pc)
                          + jnp.where(lane_lo_row, row(ROW_YS, 2 * j), row(ROW_YS, 2 * j + 1)))
            ma = (cb * jnp.exp2(jnp.where(mask, pa - row(ROW_Q, 2 * j), -jnp.inf))).astype(BF16)
            mc = (cb * jnp.exp2(jnp.where(mask, pc - row(ROW_Q, 2 * j + 1), -jnp.inf))).astype(BF16)
            tiles.append(y_off[:, 128 * j:128 * (j + 1)] * ys
                         + jnp.dot(ma, xp * xs_lo, preferred_element_type=F32)
                         + jnp.dot(mc, xp * xs_hi, preferred_element_type=F32))
        y = jnp.concatenate(tiles, axis=1)

        per_head = lambda off: jnp.concatenate(
            [jnp.broadcast_to(row(off, r), (SSM_HEAD_DIM, CHUNK)) for r in range(HEADS_PER_GROUP)], axis=0)
        xw = (xt_ref[:, pl.ds(r0, CHUNK)].astype(F32) * per_head(ROW_W)).astype(BF16)
        new = jnp.dot(xw, bm, preferred_element_type=F32)
        st_ref[direction] = state_t * per_head(ROW_CD) + new
        return r0, xs, y

    def finish(r0, xs, y):
        y = y + y_ref[pl.ds(r0, CHUNK), :] + d_ref[...] * xs.astype(F32)
        y = y * _silu(in_ref[pl.ds(r0, CHUNK), :GROUP_X].astype(F32))
        ms = jnp.mean(y * y, axis=-1, keepdims=True)
        o_ref[pl.ds(r0, CHUNK), :] = (y * lax.rsqrt(ms + NORM_EPS) * nw_ref[...]).astype(o_ref.dtype)

    st_ref[...] = jnp.zeros_like(st_ref)

    def first_visits(i, carry):
        for c, direction in ((i, 0), (nchunks - 1 - i, 1)):
            r0, _, y = chunk(c, direction)
            y_ref[pl.ds(r0, CHUNK), :] = y
        return carry

    def second_visits(i, carry):
        for c, direction in ((i, 0), (nchunks - 1 - i, 1)):
            finish(*chunk(c, direction))
        return carry

    stage_ahead(0, 0)
    stage_ahead(nchunks - 1, 1)
    lax.fori_loop(0, half, first_visits, 0)
    lax.fori_loop(half, nchunks, second_visits, 0)


def _ssd(proj, dt_t, conv_w, conv_b, head_params, d_exp, norm_w, *, batch, seq):
    g = SSM_GROUPS
    width = conv_w.shape[1]
    assert width == CONV_TAPS and (seq // CHUNK) % 2 == 0
    sel = jnp.asarray(_selection_matrices(), BF16)
    shift = jnp.asarray(_shift_matrix(), BF16)
    return pl.pallas_call(
        _ssd_kernel,
        grid=(batch, g),
        in_specs=[
            pl.BlockSpec((seq, GROUP_COLS), lambda b, k: (b, k + 1)),
            pl.BlockSpec((HEAD_ROWS, seq), lambda b, k: (k, b)),
            pl.BlockSpec((None, width, GROUP_XBC), lambda b, k: (k, 0, 0)),
            pl.BlockSpec((None, 1, GROUP_XBC), lambda b, k: (k, 0, 0)),
            pl.BlockSpec((None, HEAD_ROWS, 4), lambda b, k: (k, 0, 0)),
            pl.BlockSpec((None, 1, GROUP_X), lambda b, k: (k, 0, 0)),
            pl.BlockSpec((None, 1, GROUP_X), lambda b, k: (k, 0, 0)),
            pl.BlockSpec(sel.shape, lambda b, k: (0, 0, 0)),
            pl.BlockSpec(shift.shape, lambda b, k: (0, 0)),
        ],
        out_specs=pl.BlockSpec((seq, GROUP_X), lambda b, k: (b, k)),
        out_shape=jax.ShapeDtypeStruct((batch * seq, g * GROUP_X), BF16),
        scratch_shapes=[
            pltpu.VMEM((seq, GROUP_XBC), BF16),
            pltpu.VMEM((GROUP_X, seq), BF16),
            pltpu.VMEM((seq, CHUNK), BF16),
            pltpu.VMEM((2 * ROW_TABLES, seq // CHUNK * HEAD_ROWS, CHUNK), F32),
            pltpu.VMEM((seq, GROUP_X), F32),
            pltpu.VMEM((2, GROUP_X, SSM_STATE), F32),
            pltpu.VMEM((2, CHUNK, SEL_COLS), F32),
            pltpu.VMEM((2, CHUNK, CHUNK), F32),
        ],
        compiler_params=_params(("parallel", "arbitrary")),
        name="ssd",
    )(proj, dt_t, conv_w, conv_b, head_params, d_exp, norm_w, sel, shift)


def _out_proj_kernel(x_ref, a_ref, b_ref, wa_ref, wb_ref, o_ref):
    acc = jnp.dot(a_ref[...], wa_ref[...], preferred_element_type=F32)
    acc = acc + jnp.dot(b_ref[...], wb_ref[...], preferred_element_type=F32)
    o_ref[...] = x_ref[...] + acc


def _out_proj(x2, a_out, b_out, w_a, w_b, *, tm, tn):
    t, d = x2.shape
    ka, kb = a_out.shape[1], b_out.shape[1]
    return pl.pallas_call(
        _out_proj_kernel,
        grid=(t // tm, d // tn),
        in_specs=[
            pl.BlockSpec((tm, tn), lambda i, j: (i, j)),
            pl.BlockSpec((tm, ka), lambda i, j: (i, 0)),
            pl.BlockSpec((tm, kb), lambda i, j: (i, 0)),
            pl.BlockSpec((ka, tn), lambda i, j: (0, j)),
            pl.BlockSpec((kb, tn), lambda i, j: (0, j)),
        ],
        out_specs=pl.BlockSpec((tm, tn), lambda i, j: (i, j)),
        out_shape=jax.ShapeDtypeStruct((t, d), F32),
        compiler_params=_params(("parallel", "arbitrary")),
        name="out_proj",
    )(x2, a_out, b_out, w_a, w_b)


FFN_HALO = HALO


def _ffn_up_kernel(x_ref, xp_ref, xn_ref, g_ref, wg_ref, wv_ref, cwg_ref, cwv_ref, cbg_ref, cbv_ref,
                   o_ref, h_ref, *, tiles_per_seq):
    tm = x_ref.shape[0]
    i = pl.program_id(0)

    def norm(v):
        var = jnp.mean(v * v, axis=-1, keepdims=True)
        return v * lax.rsqrt(var + NORM_EPS) * g_ref[...]

    @pl.when(pl.program_id(1) == 0)
    def _():
        first = (i % tiles_per_seq) == 0
        last = (i % tiles_per_seq) == tiles_per_seq - 1
        h_ref[0:FFN_HALO, :] = jnp.where(first, 0.0, norm(xp_ref[...])).astype(BF16)
        h_ref[FFN_HALO + tm:, :] = jnp.where(last, 0.0, norm(xn_ref[...])).astype(BF16)

        def body(r, carry):
            r0 = pl.multiple_of(r * 128, 128)
            h_ref[pl.ds(FFN_HALO + r0, 128), :] = norm(x_ref[pl.ds(r0, 128), :]).astype(BF16)
            return carry
        lax.fori_loop(0, tm // 128, body, 0)

    rows = tm + 2 * FFN_HALO

    def conv(w_ref, cw_ref, cb_ref):
        up = jnp.dot(h_ref[...], w_ref[...], preferred_element_type=F32)
        acc = up[FFN_HALO:FFN_HALO + tm, :] * cw_ref[1:2, :] + cb_ref[...]
        acc = acc + pltpu.roll(up, 1, axis=0)[FFN_HALO:FFN_HALO + tm, :] * cw_ref[0:1, :]
        acc = acc + pltpu.roll(up, rows - 1, axis=0)[FFN_HALO:FFN_HALO + tm, :] * cw_ref[2:3, :]
        return acc

    gate = conv(wg_ref, cwg_ref, cbg_ref)
    val = conv(wv_ref, cwv_ref, cbv_ref)
    o_ref[...] = (_silu(gate) * val).astype(o_ref.dtype)


def _ffn_up(x1, gain, w_up, conv_w, conv_b, *, seq, tm, tn):
    t, d = x1.shape
    f = w_up.shape[1] // 2
    nj = f // tn
    tiles_per_seq = seq // tm
    hb = tm // FFN_HALO
    nhb = t // FFN_HALO
    return pl.pallas_call(
        functools.partial(_ffn_up_kernel, tiles_per_seq=tiles_per_seq),
        grid=(t // tm, nj),
        in_specs=[
            pl.BlockSpec((tm, d), lambda i, j: (i, 0)),
            pl.BlockSpec((FFN_HALO, d), lambda i, j: (jnp.maximum(i * hb - 1, 0), 0)),
            pl.BlockSpec((FFN_HALO, d), lambda i, j: (jnp.minimum((i + 1) * hb, nhb - 1), 0)),
            pl.BlockSpec((1, d), lambda i, j: (0, 0)),
            pl.BlockSpec((d, tn), lambda i, j: (0, j)),
            pl.BlockSpec((d, tn), lambda i, j: (0, j + nj)),
            pl.BlockSpec((3, tn), lambda i, j: (0, j)),
            pl.BlockSpec((3, tn), lambda i, j: (0, j + nj)),
            pl.BlockSpec((1, tn), lambda i, j: (0, j)),
            pl.BlockSpec((1, tn), lambda i, j: (0, j + nj)),
        ],
        out_specs=pl.BlockSpec((tm, tn), lambda i, j: (i, j)),
        out_shape=jax.ShapeDtypeStruct((t, f), BF16),
        scratch_shapes=[pltpu.VMEM((tm + 2 * FFN_HALO, d), BF16)],
        compiler_params=_params(("parallel", "arbitrary")),
        name="ffn_up",
    )(x1, x1, x1, gain, w_up, w_up, conv_w, conv_w, conv_b, conv_b)


def _ffn_down_kernel(x_ref, a_ref, w_ref, g_ref, o_ref, acc_ref):
    k = pl.program_id(1)

    @pl.when(k == 0)
    def _():
        acc_ref[...] = x_ref[...]

    acc_ref[...] += jnp.dot(a_ref[...], w_ref[...], preferred_element_type=F32)

    @pl.when(k == pl.num_programs(1) - 1)
    def _():
        v = acc_ref[...]
        var = jnp.mean(v * v, axis=-1, keepdims=True)
        o_ref[...] = v * lax.rsqrt(var + NORM_EPS) * g_ref[...]


def _ffn_down(x1, act, w_down, gain, *, tm, tk):
    t, d = x1.shape
    f = act.shape[1]
    return pl.pallas_call(
        _ffn_down_kernel,
        grid=(t // tm, f // tk),
        in_specs=[
            pl.BlockSpec((tm, d), lambda i, k: (i, 0)),
            pl.BlockSpec((tm, tk), lambda i, k: (i, k)),
            pl.BlockSpec((tk, d), lambda i, k: (k, 0)),
            pl.BlockSpec((1, d), lambda i, k: (0, 0)),
        ],
        out_specs=pl.BlockSpec((tm, d), lambda i, k: (i, 0)),
        out_shape=jax.ShapeDtypeStruct((t, d), F32),
        scratch_shapes=[pltpu.VMEM((tm, d), F32)],
        compiler_params=_params(("parallel", "arbitrary")),
        name="ffn_down",
    )(x1, act, w_down, gain)


def _layer(x2, batch, seq, norm_mix_w, w_in, fourier_w, ssm_conv_w, ssm_conv_b, dt_bias_fwd, a_log_fwd,
           dt_bias_bwd, a_log_bwd, ssm_d, ssm_norm_w, w_out, norm_ffn_w, w_up, ffn_conv_w, ffn_conv_b,
           w_down, final_gain, tiles):
    d = x2.shape[1]
    g, r = SSM_GROUPS, HEADS_PER_GROUP
    fw = fourier_w.shape[0] * fourier_w.shape[1]
    sw = g * GROUP_X
    gn = g * SSM_STATE
    o_z, o_x = fw, fw + sw
    o_b, o_c, o_dt = o_x + sw, o_x + sw + gn, o_x + sw + 2 * gn

    def per_group(lo, wdt):
        return w_in[:, lo:lo + g * wdt].reshape(d, g, wdt)
    w_groups = jnp.concatenate([per_group(o_z, GROUP_X), per_group(o_x, GROUP_X),
                                per_group(o_b, SSM_STATE), per_group(o_c, SSM_STATE)], axis=2)
    w_main = jnp.concatenate([w_in[:, :fw], w_groups.reshape(d, g * GROUP_COLS)], axis=1).astype(BF16)
    wdt = jnp.pad(w_in[:, o_dt:].reshape(d, g, r), ((0, 0), (0, 0), (0, HEAD_ROWS - r)))
    wdt = jnp.pad(wdt.reshape(d, g * HEAD_ROWS), ((0, 0), (0, 128 - g * HEAD_ROWS))).astype(BF16)

    proj, dt_t = _in_proj(x2, norm_mix_w[None, :], w_main, wdt, nh=g * HEAD_ROWS,
                          tm=tiles["in_tm"], tn=tiles["in_tn"])

    c = fourier_w.shape[1]
    cos_s, sin_s = _dft_tables(seq, 64)
    cos_c, sin_c = _dft_tables(c, 8)
    a_out = _fourier(cos_s.astype(BF16), sin_s.astype(BF16), proj, cos_c.astype(BF16), sin_c.astype(BF16),
                     fourier_w.astype(BF16), batch=batch, seq=seq, tm=tiles["f_tm"])

    def conv_group(v):
        lead = v.shape[:-1]
        parts = [v[..., :sw].reshape(*lead, g, GROUP_X), v[..., sw:sw + gn].reshape(*lead, g, SSM_STATE),
                 v[..., sw + gn:].reshape(*lead, g, SSM_STATE)]
        return jnp.moveaxis(jnp.concatenate(parts, axis=-1), -2, 0)
    conv_w = conv_group(ssm_conv_w)
    conv_b = conv_group(ssm_conv_b[None, :])
    hp = jnp.stack([dt_bias_fwd, a_log_fwd, dt_bias_bwd, a_log_bwd], axis=-1).reshape(g, r, 4)
    hp = jnp.pad(hp, ((0, 0), (0, HEAD_ROWS - r), (0, 0)))
    d_exp = jnp.repeat(ssm_d, SSM_HEAD_DIM).reshape(g, 1, GROUP_X)
    b_out = _ssd(proj, dt_t, conv_w, conv_b, hp, d_exp, ssm_norm_w.reshape(g, 1, GROUP_X),
                 batch=batch, seq=seq)

    w_out_b = w_out.astype(BF16)
    x1 = _out_proj(x2, a_out, b_out, w_out_b[:fw], w_out_b[fw:], tm=tiles["o_tm"], tn=tiles["o_tn"])

    act = _ffn_up(x1, norm_ffn_w[None, :], w_up.astype(BF16), ffn_conv_w, ffn_conv_b[None, :],
                  seq=seq, tm=tiles["u_tm"], tn=tiles["u_tn"])
    return _ffn_down(x1, act, w_down.astype(BF16), final_gain[None, :], tm=tiles["d_tm"], tk=tiles["d_tk"])


TILES = dict(in_tm=1024, in_tn=1024, f_tm=512, o_tm=1024, o_tn=512, u_tm=1024, u_tn=512, d_tm=1024, d_tk=512)


def kernel(x, norm_mix_w, w_in, fourier_w, ssm_conv_w, ssm_conv_b, dt_bias_fwd, a_log_fwd, dt_bias_bwd,
           a_log_bwd, ssm_d, ssm_norm_w, w_out, norm_ffn_w, w_up, ffn_conv_w, ffn_conv_b, w_down,
           norm_final_w):
    batch, seq, d = x.shape
    assert norm_mix_w.shape[0] == 1, "one layer"
    out = _layer(x.reshape(batch * seq, d), batch, seq, norm_mix_w[0], w_in[0], fourier_w[0], ssm_conv_w[0],
                 ssm_conv_b[0], dt_bias_fwd[0], a_log_fwd[0], dt_bias_bwd[0], a_log_bwd[0], ssm_d[0],
                 ssm_norm_w[0], w_out[0], norm_ffn_w[0], w_up[0], ffn_conv_w[0], ffn_conv_b[0], w_down[0],
                 norm_final_w, TILES)
    return out.reshape(batch, seq, d)
```

```python
import functools

import jax
import jax.numpy as jnp
import numpy as np
from jax import lax
from jax.experimental import pallas as pl
from jax.experimental.pallas import tpu as pltpu

NORM_EPS = 1e-5
FOURIER_GROUPS = 8
SSM_GROUPS = 8
HEADS_PER_GROUP = 6
HEAD_ROWS = 8
SSM_HEAD_DIM = 64
SSM_STATE = 128
CHUNK = 128
GROUP_X = HEADS_PER_GROUP * SSM_HEAD_DIM
GROUP_XBC = GROUP_X + 2 * SSM_STATE
GROUP_COLS = GROUP_X + GROUP_XBC
HALO = 16
VMEM_LIMIT = 56 * 1024 * 1024

F32 = jnp.float32
BF16 = jnp.bfloat16


def _params(semantics):
    return pltpu.CompilerParams(dimension_semantics=semantics, vmem_limit_bytes=VMEM_LIMIT)


def _silu(v):
    return v / (1.0 + jnp.exp(-v))


def _in_proj_kernel(x_ref, g_ref, w_ref, wdt_ref, o_ref, dt_ref, h_ref, *, sub):
    @pl.when(pl.program_id(1) == 0)
    def _():
        def body(r, carry):
            r0 = pl.multiple_of(r * sub, sub)
            xs = x_ref[pl.ds(r0, sub), :]
            var = jnp.mean(xs * xs, axis=-1, keepdims=True)
            hs = xs * lax.rsqrt(var + NORM_EPS) * g_ref[...]
            h_ref[pl.ds(r0, sub), :] = hs.astype(BF16)
            return carry
        lax.fori_loop(0, x_ref.shape[0] // sub, body, 0)
        dt = jnp.dot(h_ref[...], wdt_ref[...], preferred_element_type=F32)
        lanes = wdt_ref.shape[1]
        for r in range(x_ref.shape[0] // lanes):
            dt_ref[:, r * lanes:(r + 1) * lanes] = jnp.transpose(
                dt[r * lanes:(r + 1) * lanes, :])[:dt_ref.shape[0], :]

    o_ref[...] = jnp.dot(h_ref[...], w_ref[...], preferred_element_type=F32).astype(o_ref.dtype)


def _in_proj(x2, gain, w, wdt, *, nh, tm, tn):
    t, d = x2.shape
    n = w.shape[1]
    return pl.pallas_call(
        functools.partial(_in_proj_kernel, sub=128),
        grid=(t // tm, n // tn),
        in_specs=[
            pl.BlockSpec((tm, d), lambda i, j: (i, 0)),
            pl.BlockSpec((1, d), lambda i, j: (0, 0)),
            pl.BlockSpec((d, tn), lambda i, j: (0, j)),
            pl.BlockSpec(wdt.shape, lambda i, j: (0, 0)),
        ],
        out_specs=[
            pl.BlockSpec((tm, tn), lambda i, j: (i, j)),
            pl.BlockSpec((nh, tm), lambda i, j: (0, i)),
        ],
        out_shape=[jax.ShapeDtypeStruct((t, n), BF16), jax.ShapeDtypeStruct((nh, t), F32)],
        scratch_shapes=[pltpu.VMEM((tm, d), BF16)],
        compiler_params=_params(("parallel", "arbitrary")),
        name="in_proj",
    )(x2, gain, w, wdt)


def _fourier_kernel(cos_ref, sin_ref, u_ref, cc_ref, sc_ref, w_ref, o_ref, *, scale):
    u = u_ref[...]
    uc = jnp.dot(cos_ref[...], u, preferred_element_type=F32)
    us = jnp.dot(sin_ref[...], u, preferred_element_type=F32)
    c = cc_ref.shape[0]
    for g in range(w_ref.shape[0]):
        ucg = uc[:, g * c:(g + 1) * c].astype(BF16)
        usg = us[:, g * c:(g + 1) * c].astype(BF16)
        re = (jnp.dot(ucg, cc_ref[...], preferred_element_type=F32)
              - jnp.dot(usg, sc_ref[...], preferred_element_type=F32)) * scale
        o_ref[:, g * c:(g + 1) * c] = jnp.dot(
            re.astype(BF16), w_ref[g], preferred_element_type=F32).astype(o_ref.dtype)


def _fourier(cos_m, sin_m, proj, cc, sc, w_mix, *, batch, seq, tm):
    g, c, _ = w_mix.shape
    width = g * c
    scale = float(1.0 / np.sqrt(seq * c))
    nt = seq // tm
    return pl.pallas_call(
        functools.partial(_fourier_kernel, scale=scale),
        grid=(batch, nt),
        in_specs=[
            pl.BlockSpec((tm, seq), lambda b, i: (i, 0)),
            pl.BlockSpec((tm, seq), lambda b, i: (i, 0)),
            pl.BlockSpec((seq, width), lambda b, i: (b, 0)),
            pl.BlockSpec((c, c), lambda b, i: (0, 0)),
            pl.BlockSpec((c, c), lambda b, i: (0, 0)),
            pl.BlockSpec((g, c, c), lambda b, i: (0, 0, 0)),
        ],
        out_specs=pl.BlockSpec((tm, width), lambda b, i: (b * nt + i, 0)),
        out_shape=jax.ShapeDtypeStruct((batch * seq, width), BF16),
        compiler_params=_params(("parallel", "arbitrary")),
        name="fourier",
    )(cos_m, sin_m, proj, cc, sc, w_mix)


FFT_SPLIT = 4
FFT_COLS = 256


def _fourier_fft_kernel(u_ref, cc_ref, sc_ref, w_ref, l2_ref, twc_ref, tws_ref, o_ref, z_ref, *, scale):
    n2 = l2_ref.shape[0]
    c = cc_ref.shape[0]
    groups = FFT_COLS // c
    cb = pl.program_id(1)
    for gi in range(groups):
        w = w_ref[cb * groups + gi]
        mix_c = (jnp.dot(cc_ref[...], w, preferred_element_type=F32) * scale).astype(BF16)
        mix_s = (jnp.dot(sc_ref[...], w, preferred_element_type=F32) * scale).astype(BF16)
        ug = u_ref[:, gi * c:(gi + 1) * c]
        z_ref[0, :, gi * c:(gi + 1) * c] = jnp.dot(ug, mix_c, preferred_element_type=F32)
        z_ref[1, :, gi * c:(gi + 1) * c] = -jnp.dot(ug, mix_s, preferred_element_type=F32)

    zc = [z_ref[0, a * n2:(a + 1) * n2, :] for a in range(FFT_SPLIT)]
    zs = [z_ref[1, a * n2:(a + 1) * n2, :] for a in range(FFT_SPLIT)]
    ec, es, fc, fs = zc[0] + zc[2], zs[0] + zs[2], zc[0] - zc[2], zs[0] - zs[2]
    gc, gs, hc, hs = zc[1] + zc[3], zs[1] + zs[3], zc[1] - zc[3], zs[1] - zs[3]
    butterflies = [(ec + gc, es + gs), (fc + hs, fs - hc), (ec - gc, es - gs), (fc - hs, fs + hc)]
    lane_reps = FFT_COLS // twc_ref.shape[2]
    for k_lo, (yc, ys) in enumerate(butterflies):
        if k_lo > 0:
            twc = jnp.concatenate([twc_ref[k_lo - 1]] * lane_reps, axis=1)
            tws = jnp.concatenate([tws_ref[k_lo - 1]] * lane_reps, axis=1)
            yc, ys = yc * twc + ys * tws, ys * twc - yc * tws
        rhs = jnp.concatenate([yc, ys], axis=0).astype(BF16)
        o_ref[k_lo * n2:(k_lo + 1) * n2, :] = jnp.dot(
            l2_ref[...], rhs, preferred_element_type=F32).astype(o_ref.dtype)


def _fourier_fft(proj, w_mix, *, batch, seq):
    g, c, _ = w_mix.shape
    n2 = seq // FFT_SPLIT
    assert seq % FFT_SPLIT == 0 and FFT_COLS % c == 0 and (g * c) % FFT_COLS == 0
    scale = float(1.0 / np.sqrt(seq * c))
    cos_c, sin_c = _dft_tables(c, 8)
    cos_n, sin_n = _dft_tables(n2, 32)
    l2 = jnp.concatenate([cos_n, sin_n], axis=1).astype(BF16)
    k_lo = jnp.arange(1, FFT_SPLIT, dtype=jnp.int32)[:, None]
    ang = (k_lo * jnp.arange(n2, dtype=jnp.int32)[None, :]).astype(F32) * (2.0 * np.pi / seq)
    twc = jnp.broadcast_to(jnp.cos(ang)[:, :, None], (FFT_SPLIT - 1, n2, 128))
    tws = jnp.broadcast_to(jnp.sin(ang)[:, :, None], (FFT_SPLIT - 1, n2, 128))
    const2 = lambda b, j: (0, 0)
    const3 = lambda b, j: (0, 0, 0)
    out = pl.pallas_call(
        functools.partial(_fourier_fft_kernel, scale=scale),
        grid=(batch, g * c // FFT_COLS),
        in_specs=[
            pl.BlockSpec((seq, FFT_COLS), lambda b, j: (b, j)),
            pl.BlockSpec((c, c), const2),
            pl.BlockSpec((c, c), const2),
            pl.BlockSpec((g, c, c), const3),
            pl.BlockSpec(l2.shape, const2),
            pl.BlockSpec(twc.shape, const3),
            pl.BlockSpec(tws.shape, const3),
        ],
        out_specs=pl.BlockSpec((seq, FFT_COLS), lambda b, j: (b, j)),
        out_shape=jax.ShapeDtypeStruct((batch * seq, g * c), BF16),
        scratch_shapes=[pltpu.VMEM((2, seq, FFT_COLS), F32)],
        compiler_params=_params(("parallel", "arbitrary")),
        name="fourier",
    )(proj, cos_c.astype(BF16), sin_c.astype(BF16), w_mix.astype(BF16), l2, twc, tws)
    out = out.reshape(batch, FFT_SPLIT, n2, g * c)
    return jnp.transpose(out, (0, 2, 1, 3)).reshape(batch * seq, g * c)


def _dft_tables(n, split):
    hi = n // split
    s = jnp.arange(n, dtype=jnp.int32)[None, :]
    k1 = jnp.arange(hi, dtype=jnp.int32)[:, None]
    k0 = jnp.arange(split, dtype=jnp.int32)[:, None]
    a1 = ((k1 * s * split) % n).astype(F32) * (2.0 * np.pi / n)
    a0 = ((k0 * s) % n).astype(F32) * (2.0 * np.pi / n)
    c1, s1 = jnp.cos(a1)[:, None, :], jnp.sin(a1)[:, None, :]
    c0, s0 = jnp.cos(a0)[None, :, :], jnp.sin(a0)[None, :, :]
    cos_m = (c1 * c0 - s1 * s0).reshape(n, n)
    sin_m = (s1 * c0 + c1 * s0).reshape(n, n)
    return cos_m, sin_m


PIECES_P = 3
COL_DIR = PIECES_P * HEAD_ROWS
SEL_COLS = HEADS_PER_GROUP * CHUNK
ROW_Q, ROW_YS, ROW_W, ROW_CD = 0, 1, 2, 3
ROW_TABLES = 4
LOG2E = float(np.log2(np.e))
CONV_TAPS = 5
WIN_ROWS = CHUNK + 2 * HALO


def _selection_matrices():
    sel = np.zeros((2, CHUNK, SEL_COLS), np.float32)
    for d in range(2):
        for r in range(HEADS_PER_GROUP):
            for piece in range(PIECES_P):
                sel[d, d * COL_DIR + piece * HEAD_ROWS + r, r * CHUNK:(r + 1) * CHUNK] = 1.0
    return sel


def _shift_matrix():
    taps = [k for k in range(CONV_TAPS) if k != CONV_TAPS // 2]
    m = np.zeros((len(taps) * CHUNK, WIN_ROWS), np.float32)
    for i, k in enumerate(taps):
        for t in range(CHUNK):
            m[i * CHUNK + t, HALO + t + k - CONV_TAPS // 2] = 1.0
    return m


def _softplus(v):
    return jnp.maximum(v, 0.0) + jnp.log1p(jnp.exp(-jnp.abs(v)))


def _bf16_pieces(v, n):
    out = []
    for _ in range(n - 1):
        hi = v.astype(BF16).astype(F32)
        out.append(hi)
        v = v - hi
    out.append(v)
    return out


def _ssd_kernel(in_ref, dt_ref, cw_ref, cb_ref, hp_ref, d_ref, nw_ref, sel_ref, shift_ref, o_ref,
                xc_ref, xt_ref, colq_ref, rowq_ref, y_ref, st_ref, pb_ref, cbm_ref):
    seq = in_ref.shape[0]
    nchunks = seq // CHUNK
    half = nchunks // 2
    li = lax.broadcasted_iota(jnp.int32, (CHUNK, CHUNK), 0)
    si = lax.broadcasted_iota(jnp.int32, (CHUNK, CHUNK), 1)
    upper_b = jnp.where(li <= si, 1.0, 0.0).astype(BF16)
    lane_lo = si < SSM_HEAD_DIM
    lane_lo_row = lax.broadcasted_iota(jnp.int32, (1, 128), 1) < SSM_HEAD_DIM

    hp = hp_ref[...]
    bias_f, a_f = hp[:, 0:1], -jnp.exp(hp[:, 1:2])
    bias_b, a_b = hp[:, 2:3], -jnp.exp(hp[:, 3:4])

    def prep(c):
        r0 = pl.multiple_of(c * CHUNK, CHUNK)
        cur = in_ref[pl.ds(r0, CHUNK), GROUP_X:]
        p0 = pl.multiple_of(jnp.maximum(r0 - HALO, 0), HALO)
        n0 = pl.multiple_of(jnp.minimum(r0 + CHUNK, seq - HALO), HALO)
        prev = in_ref[pl.ds(p0, HALO), GROUP_X:]
        nxt = in_ref[pl.ds(n0, HALO), GROUP_X:]
        prev = jnp.where(c > 0, prev, jnp.zeros_like(prev))
        nxt = jnp.where(c < nchunks - 1, nxt, jnp.zeros_like(nxt))
        win = jnp.concatenate([prev, cur, nxt], axis=0)
        taps = jnp.dot(shift_ref[...], win, preferred_element_type=F32)
        mid = CONV_TAPS // 2
        acc = cur.astype(F32) * cw_ref[mid:mid + 1, :] + cb_ref[...]
        for i, k in enumerate([k for k in range(CONV_TAPS) if k != mid]):
            acc = acc + taps[i * CHUNK:(i + 1) * CHUNK, :] * cw_ref[k:k + 1, :]
        act = _silu(acc)
        xc_ref[pl.ds(r0, CHUNK), :] = act.astype(BF16)
        xt_ref[:, pl.ds(r0, CHUNK)] = jnp.concatenate(
            [jnp.transpose(act[:, 128 * j:128 * (j + 1)]) for j in range(GROUP_X // 128)], axis=0).astype(BF16)

    def prep_pair(i, carry):
        prep(2 * i)
        prep(2 * i + 1)
        return carry

    lax.fori_loop(0, half, prep_pair, 0)

    trows = nchunks * HEAD_ROWS
    dt_all = dt_ref[...]
    dtr = jnp.concatenate([dt_all[:, c * CHUNK:(c + 1) * CHUNK] for c in range(nchunks)], axis=0)
    per_row = lambda col: jnp.concatenate([jnp.broadcast_to(col, (HEAD_ROWS, CHUNK))] * nchunks, axis=0)
    lane_const = lambda v: jnp.broadcast_to(v, (trows, CHUNK))

    def cumsum(v):
        stacked = jnp.concatenate(_bf16_pieces(v, 3), axis=0).astype(BF16)
        out = jnp.dot(stacked, upper_b, preferred_element_type=F32)
        return (out[:trows] + out[trows:2 * trows]) + out[2 * trows:]

    dt_f = _softplus(dtr + per_row(bias_f))
    dt_b = _softplus(dtr + per_row(bias_b))
    adt_f = dt_f * per_row(a_f)
    adt_b = dt_b * per_row(a_b)
    cum_f = cumsum(adt_f)
    cum_b = cumsum(adt_b)
    tot_f = cum_f[:, CHUNK - 1:CHUNK]
    tot_b = cum_b[:, CHUNK - 1:CHUNK]
    ex_b = cum_b - adt_b
    p2_f = cum_f * LOG2E
    p2_b = -ex_b * LOG2E
    tables = [p2_f - jnp.log(dt_f) * LOG2E, jnp.zeros((trows, CHUNK), F32),
              jnp.exp(tot_f - cum_f) * dt_f, lane_const(jnp.exp(tot_f)),
              p2_b - jnp.log(dt_b) * LOG2E, lane_const(tot_b * LOG2E),
              jnp.exp(ex_b) * dt_b, lane_const(jnp.exp(tot_b))]
    for q, tab in enumerate(tables):
        rowq_ref[q] = tab
    pieces = _bf16_pieces(p2_f, PIECES_P) + _bf16_pieces(p2_b, PIECES_P)
    pad = jnp.zeros((CHUNK - 2 * COL_DIR, CHUNK), F32)
    for c in range(nchunks):
        rows = [p[c * HEAD_ROWS:(c + 1) * HEAD_ROWS, :] for p in pieces] + [pad]
        colq_ref[c * CHUNK:(c + 1) * CHUNK, :] = jnp.transpose(jnp.concatenate(rows, axis=0)).astype(BF16)

    nt = (((1,), (1,)), ((), ()))

    def stage_ahead(c, direction):
        r0 = pl.multiple_of(c * CHUNK, CHUNK)
        pb_ref[direction] = jnp.dot(colq_ref[pl.ds(r0, CHUNK), :], sel_ref[direction],
                                    preferred_element_type=F32)
        cbm_ref[direction] = lax.dot_general(
            xc_ref[pl.ds(r0, CHUNK), GROUP_X + SSM_STATE:], xc_ref[pl.ds(r0, CHUNK), GROUP_X:GROUP_X + SSM_STATE],
            nt, preferred_element_type=F32)

    def chunk(c, direction):
        r0 = pl.multiple_of(c * CHUNK, CHUNK)
        xcb = xc_ref[pl.ds(r0, CHUNK), :]
        xs = xcb[:, :GROUP_X]
        bm = xcb[:, GROUP_X:GROUP_X + SSM_STATE]
        cm = xcb[:, GROUP_X + SSM_STATE:]
        r8 = pl.multiple_of(c * HEAD_ROWS, HEAD_ROWS)
        tabs = [rowq_ref[direction * ROW_TABLES + q, pl.ds(r8, HEAD_ROWS), :] for q in range(ROW_TABLES)]
        row = lambda q, r: tabs[q][r:r + 1, :]
        pb = pb_ref[direction]
        cb = cbm_ref[direction]
        step = 1 if direction == 0 else -1
        stage_ahead(jnp.clip(c + step, 0, nchunks - 1), direction)
        mask = (li >= si) if direction == 0 else (si >= li)

        state_t = st_ref[direction]
        y_off = lax.dot_general(cm, state_t.astype(BF16), nt, preferred_element_type=F32)

        xs_lo = jnp.where(lane_lo_row, 1.0, 0.0).astype(BF16)
        xs_hi = jnp.where(lane_lo_row, 0.0, 1.0).astype(BF16)
        tiles = []
        for j in range(HEADS_PER_GROUP // 2):
            xp = xs[:, 128 * j:128 * (j + 1)]
            pa = pb[:, 2 * j * CHUNK:(2 * j + 1) * CHUNK]
            pc = pb[:, (2 * j + 1) * CHUNK:(2 * j + 2) * CHUNK]
            ys = jnp.exp2(jnp.where(lane_lo, pa, pc)
                          + jnp.where(lane_lo_row, row(ROW_YS, 2 * j), row(ROW_YS, 2 * j + 1)))
            ma = (cb * jnp.exp2(jnp.where(mask, pa - row(ROW_Q, 2 * j), -jnp.inf))).astype(BF16)
            mc = (cb * jnp.exp2(jnp.where(mask, pc - row(ROW_Q, 2 * j + 1), -jnp.inf))).astype(BF16)
            tiles.append(y_off[:, 128 * j:128 * (j + 1)] * ys
                         + jnp.dot(ma, xp * xs_lo, preferred_element_type=F32)
                         + jnp.dot(mc, xp * xs_hi, preferred_element_type=F32))
        y = jnp.concatenate(tiles, axis=1)

        per_head = lambda off, dtype=F32: jnp.concatenate(
            [jnp.broadcast_to(row(off, r).astype(dtype), (SSM_HEAD_DIM, CHUNK)) for r in range(HEADS_PER_GROUP)],
            axis=0)
        xw = xt_ref[:, pl.ds(r0, CHUNK)] * per_head(ROW_W, BF16)
        new = jnp.dot(xw, bm, preferred_element_type=F32)
        st_ref[direction] = state_t * per_head(ROW_CD) + new
        return r0, xs, y

    def finish(r0, xs, y):
        y = y + y_ref[pl.ds(r0, CHUNK), :] + d_ref[...] * xs.astype(F32)
        y = y * _silu(in_ref[pl.ds(r0, CHUNK), :GROUP_X].astype(F32))
        ms = jnp.mean(y * y, axis=-1, keepdims=True)
        o_ref[pl.ds(r0, CHUNK), :] = (y * lax.rsqrt(ms + NORM_EPS) * nw_ref[...]).astype(o_ref.dtype)

    st_ref[...] = jnp.zeros_like(st_ref)

    def first_visits(i, carry):
        for c, direction in ((i, 0), (nchunks - 1 - i, 1)):
            r0, _, y = chunk(c, direction)
            y_ref[pl.ds(r0, CHUNK), :] = y
        return carry

    def second_visits(i, carry):
        for c, direction in ((i, 0), (nchunks - 1 - i, 1)):
            finish(*chunk(c, direction))
        return carry

    stage_ahead(0, 0)
    stage_ahead(nchunks - 1, 1)
    lax.fori_loop(0, half, first_visits, 0)
    lax.fori_loop(half, nchunks, second_visits, 0)


def _ssd(proj, dt_t, conv_w, conv_b, head_params, d_exp, norm_w, *, batch, seq):
    g = SSM_GROUPS
    width = conv_w.shape[1]
    assert width == CONV_TAPS and (seq // CHUNK) % 2 == 0
    sel = jnp.asarray(_selection_matrices(), BF16)
    shift = jnp.asarray(_shift_matrix(), BF16)
    return pl.pallas_call(
        _ssd_kernel,
        grid=(batch, g),
        in_specs=[
            pl.BlockSpec((seq, GROUP_COLS), lambda b, k: (b, k + 1)),
            pl.BlockSpec((HEAD_ROWS, seq), lambda b, k: (k, b)),
            pl.BlockSpec((None, width, GROUP_XBC), lambda b, k: (k, 0, 0)),
            pl.BlockSpec((None, 1, GROUP_XBC), lambda b, k: (k, 0, 0)),
            pl.BlockSpec((None, HEAD_ROWS, 4), lambda b, k: (k, 0, 0)),
            pl.BlockSpec((None, 1, GROUP_X), lambda b, k: (k, 0, 0)),
            pl.BlockSpec((None, 1, GROUP_X), lambda b, k: (k, 0, 0)),
            pl.BlockSpec(sel.shape, lambda b, k: (0, 0, 0)),
            pl.BlockSpec(shift.shape, lambda b, k: (0, 0)),
        ],
        out_specs=pl.BlockSpec((seq, GROUP_X), lambda b, k: (b, k)),
        out_shape=jax.ShapeDtypeStruct((batch * seq, g * GROUP_X), BF16),
        scratch_shapes=[
            pltpu.VMEM((seq, GROUP_XBC), BF16),
            pltpu.VMEM((GROUP_X, seq), BF16),
            pltpu.VMEM((seq, CHUNK), BF16),
            pltpu.VMEM((2 * ROW_TABLES, seq // CHUNK * HEAD_ROWS, CHUNK), F32),
            pltpu.VMEM((seq, GROUP_X), F32),
            pltpu.VMEM((2, GROUP_X, SSM_STATE), F32),
            pltpu.VMEM((2, CHUNK, SEL_COLS), F32),
            pltpu.VMEM((2, CHUNK, CHUNK), F32),
        ],
        compiler_params=_params(("parallel", "arbitrary")),
        name="ssd",
    )(proj, dt_t, conv_w, conv_b, head_params, d_exp, norm_w, sel, shift)


def _out_proj_kernel(x_ref, a_ref, b_ref, wa_ref, wb_ref, o_ref):
    acc = jnp.dot(a_ref[...], wa_ref[...], preferred_element_type=F32)
    acc = acc + jnp.dot(b_ref[...], wb_ref[...], preferred_element_type=F32)
    o_ref[...] = x_ref[...] + acc


def _out_proj(x2, a_out, b_out, w_a, w_b, *, tm, tn):
    t, d = x2.shape
    ka, kb = a_out.shape[1], b_out.shape[1]
    return pl.pallas_call(
        _out_proj_kernel,
        grid=(t // tm, d // tn),
        in_specs=[
            pl.BlockSpec((tm, tn), lambda i, j: (i, j)),
            pl.BlockSpec((tm, ka), lambda i, j: (i, 0)),
            pl.BlockSpec((tm, kb), lambda i, j: (i, 0)),
            pl.BlockSpec((ka, tn), lambda i, j: (0, j)),
            pl.BlockSpec((kb, tn), lambda i, j: (0, j)),
        ],
        out_specs=pl.BlockSpec((tm, tn), lambda i, j: (i, j)),
        out_shape=jax.ShapeDtypeStruct((t, d), F32),
        compiler_params=_params(("parallel", "arbitrary")),
        name="out_proj",
    )(x2, a_out, b_out, w_a, w_b)


FFN_HALO = HALO


def _ffn_up_kernel(x_ref, xp_ref, xn_ref, g_ref, wg_ref, wv_ref, cwg_ref, cwv_ref, cbg_ref, cbv_ref,
                   o_ref, h_ref, *, tiles_per_seq):
    tm = x_ref.shape[0]
    i = pl.program_id(0)

    def norm(v):
        var = jnp.mean(v * v, axis=-1, keepdims=True)
        return v * lax.rsqrt(var + NORM_EPS) * g_ref[...]

    @pl.when(pl.program_id(1) == 0)
    def _():
        first = (i % tiles_per_seq) == 0
        last = (i % tiles_per_seq) == tiles_per_seq - 1
        h_ref[0:FFN_HALO, :] = jnp.where(first, 0.0, norm(xp_ref[...])).astype(BF16)
        h_ref[FFN_HALO + tm:, :] = jnp.where(last, 0.0, norm(xn_ref[...])).astype(BF16)

        def body(r, carry):
            r0 = pl.multiple_of(r * 128, 128)
            h_ref[pl.ds(FFN_HALO + r0, 128), :] = norm(x_ref[pl.ds(r0, 128), :]).astype(BF16)
            return carry
        lax.fori_loop(0, tm // 128, body, 0)

    rows = tm + 2 * FFN_HALO

    def conv(w_ref, cw_ref, cb_ref):
        up = jnp.dot(h_ref[...], w_ref[...], preferred_element_type=F32)
        acc = up[FFN_HALO:FFN_HALO + tm, :] * cw_ref[1:2, :] + cb_ref[...]
        acc = acc + pltpu.roll(up, 1, axis=0)[FFN_HALO:FFN_HALO + tm, :] * cw_ref[0:1, :]
        acc = acc + pltpu.roll(up, rows - 1, axis=0)[FFN_HALO:FFN_HALO + tm, :] * cw_ref[2:3, :]
        return acc

    gate = conv(wg_ref, cwg_ref, cbg_ref)
    val = conv(wv_ref, cwv_ref, cbv_ref)
    o_ref[...] = (_silu(gate) * val).astype(o_ref.dtype)


def _ffn_up(x1, gain, w_up, conv_w, conv_b, *, seq, tm, tn):
    t, d = x1.shape
    f = w_up.shape[1] // 2
    nj = f // tn
    tiles_per_seq = seq // tm
    hb = tm // FFN_HALO
    nhb = t // FFN_HALO
    return pl.pallas_call(
        functools.partial(_ffn_up_kernel, tiles_per_seq=tiles_per_seq),
        grid=(t // tm, nj),
        in_specs=[
            pl.BlockSpec((tm, d), lambda i, j: (i, 0)),
            pl.BlockSpec((FFN_HALO, d), lambda i, j: (jnp.maximum(i * hb - 1, 0), 0)),
            pl.BlockSpec((FFN_HALO, d), lambda i, j: (jnp.minimum((i + 1) * hb, nhb - 1), 0)),
            pl.BlockSpec((1, d), lambda i, j: (0, 0)),
            pl.BlockSpec((d, tn), lambda i, j: (0, j)),
            pl.BlockSpec((d, tn), lambda i, j: (0, j + nj)),
            pl.BlockSpec((3, tn), lambda i, j: (0, j)),
            pl.BlockSpec((3, tn), lambda i, j: (0, j + nj)),
            pl.BlockSpec((1, tn), lambda i, j: (0, j)),
            pl.BlockSpec((1, tn), lambda i, j: (0, j + nj)),
        ],
        out_specs=pl.BlockSpec((tm, tn), lambda i, j: (i, j)),
        out_shape=jax.ShapeDtypeStruct((t, f), BF16),
        scratch_shapes=[pltpu.VMEM((tm + 2 * FFN_HALO, d), BF16)],
        compiler_params=_params(("parallel", "arbitrary")),
        name="ffn_up",
    )(x1, x1, x1, gain, w_up, w_up, conv_w, conv_w, conv_b, conv_b)


def _ffn_down_kernel(x_ref, a_ref, w_ref, g_ref, o_ref, acc_ref):
    k = pl.program_id(1)

    @pl.when(k == 0)
    def _():
        acc_ref[...] = x_ref[...]

    acc_ref[...] += jnp.dot(a_ref[...], w_ref[...], preferred_element_type=F32)

    @pl.when(k == pl.num_programs(1) - 1)
    def _():
        v = acc_ref[...]
        var = jnp.mean(v * v, axis=-1, keepdims=True)
        o_ref[...] = v * lax.rsqrt(var + NORM_EPS) * g_ref[...]


def _ffn_down(x1, act, w_down, gain, *, tm, tk):
    t, d = x1.shape
    f = act.shape[1]
    return pl.pallas_call(
        _ffn_down_kernel,
        grid=(t // tm, f // tk),
        in_specs=[
            pl.BlockSpec((tm, d), lambda i, k: (i, 0)),
            pl.BlockSpec((tm, tk), lambda i, k: (i, k)),
            pl.BlockSpec((tk, d), lambda i, k: (k, 0)),
            pl.BlockSpec((1, d), lambda i, k: (0, 0)),
        ],
        out_specs=pl.BlockSpec((tm, d), lambda i, k: (i, 0)),
        out_shape=jax.ShapeDtypeStruct((t, d), F32),
        scratch_shapes=[pltpu.VMEM((tm, d), F32)],
        compiler_params=_params(("parallel", "arbitrary")),
        name="ffn_down",
    )(x1, act, w_down, gain)


def _layer(x2, batch, seq, norm_mix_w, w_in, fourier_w, ssm_conv_w, ssm_conv_b, dt_bias_fwd, a_log_fwd,
           dt_bias_bwd, a_log_bwd, ssm_d, ssm_norm_w, w_out, norm_ffn_w, w_up, ffn_conv_w, ffn_conv_b,
           w_down, final_gain, tiles):
    d = x2.shape[1]
    g, r = SSM_GROUPS, HEADS_PER_GROUP
    fw = fourier_w.shape[0] * fourier_w.shape[1]
    sw = g * GROUP_X
    gn = g * SSM_STATE
    o_z, o_x = fw, fw + sw
    o_b, o_c, o_dt = o_x + sw, o_x + sw + gn, o_x + sw + 2 * gn

    def per_group(lo, wdt):
        return w_in[:, lo:lo + g * wdt].reshape(d, g, wdt)
    w_groups = jnp.concatenate([per_group(o_z, GROUP_X), per_group(o_x, GROUP_X),
                                per_group(o_b, SSM_STATE), per_group(o_c, SSM_STATE)], axis=2)
    w_main = jnp.concatenate([w_in[:, :fw], w_groups.reshape(d, g * GROUP_COLS)], axis=1).astype(BF16)
    wdt = jnp.pad(w_in[:, o_dt:].reshape(d, g, r), ((0, 0), (0, 0), (0, HEAD_ROWS - r)))
    wdt = jnp.pad(wdt.reshape(d, g * HEAD_ROWS), ((0, 0), (0, 128 - g * HEAD_ROWS))).astype(BF16)

    proj, dt_t = _in_proj(x2, norm_mix_w[None, :], w_main, wdt, nh=g * HEAD_ROWS,
                          tm=tiles["in_tm"], tn=tiles["in_tn"])

    a_out = _fourier_fft(proj, fourier_w, batch=batch, seq=seq)

    def conv_group(v):
        lead = v.shape[:-1]
        parts = [v[..., :sw].reshape(*lead, g, GROUP_X), v[..., sw:sw + gn].reshape(*lead, g, SSM_STATE),
                 v[..., sw + gn:].reshape(*lead, g, SSM_STATE)]
        return jnp.moveaxis(jnp.concatenate(parts, axis=-1), -2, 0)
    conv_w = conv_group(ssm_conv_w)
    conv_b = conv_group(ssm_conv_b[None, :])
    hp = jnp.stack([dt_bias_fwd, a_log_fwd, dt_bias_bwd, a_log_bwd], axis=-1).reshape(g, r, 4)
    hp = jnp.pad(hp, ((0, 0), (0, HEAD_ROWS - r), (0, 0)))
    d_exp = jnp.repeat(ssm_d, SSM_HEAD_DIM).reshape(g, 1, GROUP_X)
    b_out = _ssd(proj, dt_t, conv_w, conv_b, hp, d_exp, ssm_norm_w.reshape(g, 1, GROUP_X),
                 batch=batch, seq=seq)

    w_out_b = w_out.astype(BF16)
    x1 = _out_proj(x2, a_out, b_out, w_out_b[:fw], w_out_b[fw:], tm=tiles["o_tm"], tn=tiles["o_tn"])

    act = _ffn_up(x1, norm_ffn_w[None, :], w_up.astype(BF16), ffn_conv_w, ffn_conv_b[None, :],
                  seq=seq, tm=tiles["u_tm"], tn=tiles["u_tn"])
    return _ffn_down(x1, act, w_down.astype(BF16), final_gain[None, :], tm=tiles["d_tm"], tk=tiles["d_tk"])


TILES = dict(in_tm=1024, in_tn=1024, f_tm=512, o_tm=1024, o_tn=512, u_tm=1024, u_tn=512, d_tm=1024, d_tk=512)


def kernel(x, norm_mix_w, w_in, fourier_w, ssm_conv_w, ssm_conv_b, dt_bias_fwd, a_log_fwd, dt_bias_bwd,
           a_log_bwd, ssm_d, ssm_norm_w, w_out, norm_ffn_w, w_up, ffn_conv_w, ffn_conv_b, w_down,
           norm_final_w):
    batch, seq, d = x.shape
    assert norm_mix_w.shape[0] == 1, "one layer"
    out = _layer(x.reshape(batch * seq, d), batch, seq, norm_mix_w[0], w_in[0], fourier_w[0], ssm_conv_w[0],
                 ssm_conv_b[0], dt_bias_fwd[0], a_log_fwd[0], dt_bias_bwd[0], a_log_bwd[0], ssm_d[0],
                 ssm_norm_w[0], w_out[0], norm_ffn_w[0], w_up[0], ffn_conv_w[0], ffn_conv_b[0], w_down[0],
                 norm_final_w, TILES)
    return out.reshape(batch, seq, d)
```

```python
import functools

import jax
import jax.numpy as jnp
import numpy as np
from jax import lax
from jax.experimental import pallas as pl
from jax.experimental.pallas import tpu as pltpu

NORM_EPS = 1e-5
FOURIER_GROUPS = 8
SSM_GROUPS = 8
HEADS_PER_GROUP = 6
HEAD_ROWS = 8
SSM_HEAD_DIM = 64
SSM_STATE = 128
CHUNK = 128
GROUP_X = HEADS_PER_GROUP * SSM_HEAD_DIM
GROUP_XBC = GROUP_X + 2 * SSM_STATE
GROUP_COLS = GROUP_X + GROUP_XBC
HALO = 16
VMEM_LIMIT = 56 * 1024 * 1024

F32 = jnp.float32
BF16 = jnp.bfloat16


def _params(semantics):
    return pltpu.CompilerParams(dimension_semantics=semantics, vmem_limit_bytes=VMEM_LIMIT)


def _silu(v):
    return v / (1.0 + jnp.exp(-v))


def _in_proj_kernel(x_ref, g_ref, w_ref, wdt_ref, o_ref, dt_ref, h_ref, *, sub):
    @pl.when(pl.program_id(1) == 0)
    def _():
        def body(r, carry):
            r0 = pl.multiple_of(r * sub, sub)
            xs = x_ref[pl.ds(r0, sub), :]
            var = jnp.mean(xs * xs, axis=-1, keepdims=True)
            hs = xs * lax.rsqrt(var + NORM_EPS) * g_ref[...]
            h_ref[pl.ds(r0, sub), :] = hs.astype(BF16)
            return carry
        lax.fori_loop(0, x_ref.shape[0] // sub, body, 0)
        dt = jnp.dot(h_ref[...], wdt_ref[...], preferred_element_type=F32)
        lanes = wdt_ref.shape[1]
        for r in range(x_ref.shape[0] // lanes):
            dt_ref[:, r * lanes:(r + 1) * lanes] = jnp.transpose(
                dt[r * lanes:(r + 1) * lanes, :])[:dt_ref.shape[0], :]

    o_ref[...] = jnp.dot(h_ref[...], w_ref[...], preferred_element_type=F32).astype(o_ref.dtype)


def _in_proj(x2, gain, w, wdt, *, n, nh, tm, tn):
    t, d = x2.shape
    nj = n // tn
    return pl.pallas_call(
        functools.partial(_in_proj_kernel, sub=128),
        grid=(t // tm, nj),
        in_specs=[
            pl.BlockSpec((tm, d), lambda i, j: (i, 0)),
            pl.BlockSpec((1, d), lambda i, j: (0, 0)),
            pl.BlockSpec((d, tn), lambda i, j: (0, (j + 1) % nj)),
            pl.BlockSpec(wdt.shape, lambda i, j: (0, 0)),
        ],
        out_specs=[
            pl.BlockSpec((tm, tn), lambda i, j: (i, j)),
            pl.BlockSpec((nh, tm), lambda i, j: (0, i)),
        ],
        out_shape=[jax.ShapeDtypeStruct((t, n), BF16), jax.ShapeDtypeStruct((nh, t), F32)],
        scratch_shapes=[pltpu.VMEM((tm, d), BF16)],
        compiler_params=_params(("parallel", "arbitrary")),
        name="in_proj",
    )(x2, gain, w, wdt)


def _fourier_kernel(cos_ref, sin_ref, u_ref, cc_ref, sc_ref, w_ref, o_ref, *, scale):
    u = u_ref[...]
    uc = jnp.dot(cos_ref[...], u, preferred_element_type=F32)
    us = jnp.dot(sin_ref[...], u, preferred_element_type=F32)
    c = cc_ref.shape[0]
    for g in range(w_ref.shape[0]):
        ucg = uc[:, g * c:(g + 1) * c].astype(BF16)
        usg = us[:, g * c:(g + 1) * c].astype(BF16)
        re = (jnp.dot(ucg, cc_ref[...], preferred_element_type=F32)
              - jnp.dot(usg, sc_ref[...], preferred_element_type=F32)) * scale
        o_ref[:, g * c:(g + 1) * c] = jnp.dot(
            re.astype(BF16), w_ref[g], preferred_element_type=F32).astype(o_ref.dtype)


def _fourier(cos_m, sin_m, proj, cc, sc, w_mix, *, batch, seq, tm):
    g, c, _ = w_mix.shape
    width = g * c
    scale = float(1.0 / np.sqrt(seq * c))
    nt = seq // tm
    return pl.pallas_call(
        functools.partial(_fourier_kernel, scale=scale),
        grid=(batch, nt),
        in_specs=[
            pl.BlockSpec((tm, seq), lambda b, i: (i, 0)),
            pl.BlockSpec((tm, seq), lambda b, i: (i, 0)),
            pl.BlockSpec((seq, width), lambda b, i: (b, 0)),
            pl.BlockSpec((c, c), lambda b, i: (0, 0)),
            pl.BlockSpec((c, c), lambda b, i: (0, 0)),
            pl.BlockSpec((g, c, c), lambda b, i: (0, 0, 0)),
        ],
        out_specs=pl.BlockSpec((tm, width), lambda b, i: (b * nt + i, 0)),
        out_shape=jax.ShapeDtypeStruct((batch * seq, width), BF16),
        compiler_params=_params(("parallel", "arbitrary")),
        name="fourier",
    )(cos_m, sin_m, proj, cc, sc, w_mix)


FFT_SPLIT = 4
FFT_COLS = 256


def _fourier_fft_kernel(u_ref, cc_ref, sc_ref, w_ref, l2_ref, twc_ref, tws_ref, o_ref, z_ref, o32_ref, *, scale):
    n2 = l2_ref.shape[0]
    c = cc_ref.shape[0]
    groups = FFT_COLS // c
    cb = pl.program_id(1)
    for gi in range(groups):
        w = w_ref[cb * groups + gi]
        mix_c = (jnp.dot(cc_ref[...], w, preferred_element_type=F32) * scale).astype(BF16)
        mix_s = (jnp.dot(sc_ref[...], w, preferred_element_type=F32) * scale).astype(BF16)
        ug = u_ref[:, gi * c:(gi + 1) * c]
        z_ref[0, :, gi * c:(gi + 1) * c] = jnp.dot(ug, mix_c, preferred_element_type=F32)
        z_ref[1, :, gi * c:(gi + 1) * c] = -jnp.dot(ug, mix_s, preferred_element_type=F32)

    zc = [z_ref[0, a * n2:(a + 1) * n2, :] for a in range(FFT_SPLIT)]
    zs = [z_ref[1, a * n2:(a + 1) * n2, :] for a in range(FFT_SPLIT)]
    ec, es, fc, fs = zc[0] + zc[2], zs[0] + zs[2], zc[0] - zc[2], zs[0] - zs[2]
    gc, gs, hc, hs = zc[1] + zc[3], zs[1] + zs[3], zc[1] - zc[3], zs[1] - zs[3]
    butterflies = [(ec + gc, es + gs), (fc + hs, fs - hc), (ec - gc, es - gs), (fc - hs, fs + hc)]
    lane_reps = FFT_COLS // twc_ref.shape[2]
    for k_lo, (yc, ys) in enumerate(butterflies):
        if k_lo > 0:
            twc = jnp.concatenate([twc_ref[k_lo - 1]] * lane_reps, axis=1)
            tws = jnp.concatenate([tws_ref[k_lo - 1]] * lane_reps, axis=1)
            yc, ys = yc * twc + ys * tws, ys * twc - yc * tws
        rhs = jnp.concatenate([yc, ys], axis=0).astype(BF16)
        x = jnp.dot(l2_ref[...], rhs, preferred_element_type=F32)
        for h in range(FFT_COLS // 128):
            o32_ref[h, pl.ds(k_lo, n2, stride=FFT_SPLIT), :] = x[:, 128 * h:128 * (h + 1)]
    for h in range(FFT_COLS // 128):
        o_ref[:, 128 * h:128 * (h + 1)] = o32_ref[h].astype(o_ref.dtype)


def _fourier_fft(proj, w_mix, *, batch, seq, col0):
    g, c, _ = w_mix.shape
    n2 = seq // FFT_SPLIT
    assert seq % FFT_SPLIT == 0 and FFT_COLS % c == 0 and (g * c) % FFT_COLS == 0 and col0 % FFT_COLS == 0
    blk0 = col0 // FFT_COLS
    scale = float(1.0 / np.sqrt(seq * c))
    cos_c, sin_c = _dft_tables(c, 8)
    cos_n, sin_n = _dft_tables(n2, 32)
    l2 = jnp.concatenate([cos_n, sin_n], axis=1).astype(BF16)
    k_lo = jnp.arange(1, FFT_SPLIT, dtype=jnp.int32)[:, None]
    ang = (k_lo * jnp.arange(n2, dtype=jnp.int32)[None, :]).astype(F32) * (2.0 * np.pi / seq)
    twc = jnp.broadcast_to(jnp.cos(ang)[:, :, None], (FFT_SPLIT - 1, n2, 128))
    tws = jnp.broadcast_to(jnp.sin(ang)[:, :, None], (FFT_SPLIT - 1, n2, 128))
    const2 = lambda b, j: (0, 0)
    const3 = lambda b, j: (0, 0, 0)
    return pl.pallas_call(
        functools.partial(_fourier_fft_kernel, scale=scale),
        grid=(batch, g * c // FFT_COLS),
        in_specs=[
            pl.BlockSpec((seq, FFT_COLS), lambda b, j: (b, blk0 + j)),
            pl.BlockSpec((c, c), const2),
            pl.BlockSpec((c, c), const2),
            pl.BlockSpec((g, c, c), const3),
            pl.BlockSpec(l2.shape, const2),
            pl.BlockSpec(twc.shape, const3),
            pl.BlockSpec(tws.shape, const3),
        ],
        out_specs=pl.BlockSpec((seq, FFT_COLS), lambda b, j: (b, j)),
        out_shape=jax.ShapeDtypeStruct((batch * seq, g * c), BF16),
        scratch_shapes=[pltpu.VMEM((2, seq, FFT_COLS), F32),
                        pltpu.VMEM((FFT_COLS // 128, seq, 128), F32)],
        compiler_params=_params(("parallel", "arbitrary")),
        name="fourier",
    )(proj, cos_c.astype(BF16), sin_c.astype(BF16), w_mix.astype(BF16), l2, twc, tws)


def _dft_tables(n, split):
    hi = n // split
    s = jnp.arange(n, dtype=jnp.int32)[None, :]
    k1 = jnp.arange(hi, dtype=jnp.int32)[:, None]
    k0 = jnp.arange(split, dtype=jnp.int32)[:, None]
    a1 = ((k1 * s * split) % n).astype(F32) * (2.0 * np.pi / n)
    a0 = ((k0 * s) % n).astype(F32) * (2.0 * np.pi / n)
    c1, s1 = jnp.cos(a1)[:, None, :], jnp.sin(a1)[:, None, :]
    c0, s0 = jnp.cos(a0)[None, :, :], jnp.sin(a0)[None, :, :]
    cos_m = (c1 * c0 - s1 * s0).reshape(n, n)
    sin_m = (s1 * c0 + c1 * s0).reshape(n, n)
    return cos_m, sin_m


PIECES_P = 3
COL_DIR = PIECES_P * HEAD_ROWS
SEL_COLS = HEADS_PER_GROUP * CHUNK
ROW_Q, ROW_YS, ROW_W, ROW_CD = 0, 1, 2, 3
ROW_TABLES = 4
LOG2E = float(np.log2(np.e))
CONV_TAPS = 5
WIN_ROWS = CHUNK + 2 * HALO


def _selection_matrices():
    sel = np.zeros((2, CHUNK, SEL_COLS), np.float32)
    for d in range(2):
        for r in range(HEADS_PER_GROUP):
            for piece in range(PIECES_P):
                sel[d, d * COL_DIR + piece * HEAD_ROWS + r, r * CHUNK:(r + 1) * CHUNK] = 1.0
    return sel


def _shift_matrix():
    taps = [k for k in range(CONV_TAPS) if k != CONV_TAPS // 2]
    m = np.zeros((len(taps) * CHUNK, WIN_ROWS), np.float32)
    for i, k in enumerate(taps):
        for t in range(CHUNK):
            m[i * CHUNK + t, HALO + t + k - CONV_TAPS // 2] = 1.0
    return m


def _softplus(v):
    return jnp.maximum(v, 0.0) + jnp.log1p(jnp.exp(-jnp.abs(v)))


def _bf16_pieces(v, n):
    out = []
    for _ in range(n - 1):
        hi = v.astype(BF16).astype(F32)
        out.append(hi)
        v = v - hi
    out.append(v)
    return out


def _ssd_kernel(z_ref, x_ref, b_ref, c_ref, dt_ref, cwx_ref, cwb_ref, cwc_ref, cbx_ref, cbb_ref, cbc_ref,
                hp_ref, d_ref, nw_ref, sel_ref, shift_ref, o_ref,
                xc_ref, xt_ref, colq_ref, rowq_ref, y_ref, st_ref, pb_ref, cbm_ref):
    seq = x_ref.shape[0]
    nchunks = seq // CHUNK
    half = nchunks // 2
    xbc_rows = lambda start, n: jnp.concatenate(
        [ref[pl.ds(start, n), :] for ref in (x_ref, b_ref, c_ref)], axis=1)
    conv_w = jnp.concatenate([cwx_ref[...], cwb_ref[...], cwc_ref[...]], axis=1)
    conv_b = jnp.concatenate([cbx_ref[...], cbb_ref[...], cbc_ref[...]], axis=1)
    li = lax.broadcasted_iota(jnp.int32, (CHUNK, CHUNK), 0)
    si = lax.broadcasted_iota(jnp.int32, (CHUNK, CHUNK), 1)
    upper_b = jnp.where(li <= si, 1.0, 0.0).astype(BF16)
    lane_lo = si < SSM_HEAD_DIM
    lane_lo_row = lax.broadcasted_iota(jnp.int32, (1, 128), 1) < SSM_HEAD_DIM

    hp = hp_ref[...]
    bias_f, a_f = hp[:, 0:1], -jnp.exp(hp[:, 1:2])
    bias_b, a_b = hp[:, 2:3], -jnp.exp(hp[:, 3:4])

    def prep(c):
        r0 = pl.multiple_of(c * CHUNK, CHUNK)
        cur = xbc_rows(r0, CHUNK)
        p0 = pl.multiple_of(jnp.maximum(r0 - HALO, 0), HALO)
        n0 = pl.multiple_of(jnp.minimum(r0 + CHUNK, seq - HALO), HALO)
        prev = xbc_rows(p0, HALO)
        nxt = xbc_rows(n0, HALO)
        prev = jnp.where(c > 0, prev, jnp.zeros_like(prev))
        nxt = jnp.where(c < nchunks - 1, nxt, jnp.zeros_like(nxt))
        win = jnp.concatenate([prev, cur, nxt], axis=0)
        taps = jnp.dot(shift_ref[...], win, preferred_element_type=F32)
        mid = CONV_TAPS // 2
        acc = cur.astype(F32) * conv_w[mid:mid + 1, :] + conv_b
        for i, k in enumerate([k for k in range(CONV_TAPS) if k != mid]):
            acc = acc + taps[i * CHUNK:(i + 1) * CHUNK, :] * conv_w[k:k + 1, :]
        act = _silu(acc)
        xc_ref[pl.ds(r0, CHUNK), :] = act.astype(BF16)
        xt_ref[:, pl.ds(r0, CHUNK)] = jnp.concatenate(
            [jnp.transpose(act[:, 128 * j:128 * (j + 1)]) for j in range(GROUP_X // 128)], axis=0).astype(BF16)

    def prep_pair(i, carry):
        prep(2 * i)
        prep(2 * i + 1)
        return carry

    lax.fori_loop(0, half, prep_pair, 0)

    trows = nchunks * HEAD_ROWS
    dt_all = dt_ref[...]
    dtr = jnp.concatenate([dt_all[:, c * CHUNK:(c + 1) * CHUNK] for c in range(nchunks)], axis=0)
    per_row = lambda col: jnp.concatenate([jnp.broadcast_to(col, (HEAD_ROWS, CHUNK))] * nchunks, axis=0)
    lane_const = lambda v: jnp.broadcast_to(v, (trows, CHUNK))

    def cumsum(v):
        stacked = jnp.concatenate(_bf16_pieces(v, 3), axis=0).astype(BF16)
        out = jnp.dot(stacked, upper_b, preferred_element_type=F32)
        return (out[:trows] + out[trows:2 * trows]) + out[2 * trows:]

    dt_f = _softplus(dtr + per_row(bias_f))
    dt_b = _softplus(dtr + per_row(bias_b))
    adt_f = dt_f * per_row(a_f)
    adt_b = dt_b * per_row(a_b)
    cum_f = cumsum(adt_f)
    cum_b = cumsum(adt_b)
    tot_f = cum_f[:, CHUNK - 1:CHUNK]
    tot_b = cum_b[:, CHUNK - 1:CHUNK]
    ex_b = cum_b - adt_b
    p2_f = cum_f * LOG2E
    p2_b = -ex_b * LOG2E
    tables = [p2_f - jnp.log(dt_f) * LOG2E, jnp.zeros((trows, CHUNK), F32),
              jnp.exp(tot_f - cum_f) * dt_f, lane_const(jnp.exp(tot_f)),
              p2_b - jnp.log(dt_b) * LOG2E, lane_const(tot_b * LOG2E),
              jnp.exp(ex_b) * dt_b, lane_const(jnp.exp(tot_b))]
    for q, tab in enumerate(tables):
        rowq_ref[q] = tab
    pieces = _bf16_pieces(p2_f, PIECES_P) + _bf16_pieces(p2_b, PIECES_P)
    pad = jnp.zeros((CHUNK - 2 * COL_DIR, CHUNK), F32)
    for c in range(nchunks):
        rows = [p[c * HEAD_ROWS:(c + 1) * HEAD_ROWS, :] for p in pieces] + [pad]
        colq_ref[c * CHUNK:(c + 1) * CHUNK, :] = jnp.transpose(jnp.concatenate(rows, axis=0)).astype(BF16)

    nt = (((1,), (1,)), ((), ()))

    def stage_ahead(c, direction):
        r0 = pl.multiple_of(c * CHUNK, CHUNK)
        pb_ref[direction] = jnp.dot(colq_ref[pl.ds(r0, CHUNK), :], sel_ref[direction],
                                    preferred_element_type=F32)
        cbm_ref[direction] = lax.dot_general(
            xc_ref[pl.ds(r0, CHUNK), GROUP_X + SSM_STATE:], xc_ref[pl.ds(r0, CHUNK), GROUP_X:GROUP_X + SSM_STATE],
            nt, preferred_element_type=F32)

    def chunk(c, direction):
        r0 = pl.multiple_of(c * CHUNK, CHUNK)
        xcb = xc_ref[pl.ds(r0, CHUNK), :]
        xs = xcb[:, :GROUP_X]
        bm = xcb[:, GROUP_X:GROUP_X + SSM_STATE]
        cm = xcb[:, GROUP_X + SSM_STATE:]
        r8 = pl.multiple_of(c * HEAD_ROWS, HEAD_ROWS)
        tabs = [rowq_ref[direction * ROW_TABLES + q, pl.ds(r8, HEAD_ROWS), :] for q in range(ROW_TABLES)]
        row = lambda q, r: tabs[q][r:r + 1, :]
        pb = pb_ref[direction]
        cb = cbm_ref[direction]
        step = 1 if direction == 0 else -1
        stage_ahead(jnp.clip(c + step, 0, nchunks - 1), direction)
        mask = (li >= si) if direction == 0 else (si >= li)

        state_t = st_ref[direction]
        y_off = lax.dot_general(cm, state_t.astype(BF16), nt, preferred_element_type=F32)

        xs_lo = jnp.where(lane_lo_row, 1.0, 0.0).astype(BF16)
        xs_hi = jnp.where(lane_lo_row, 0.0, 1.0).astype(BF16)
        tiles = []
        for j in range(HEADS_PER_GROUP // 2):
            xp = xs[:, 128 * j:128 * (j + 1)]
            pa = pb[:, 2 * j * CHUNK:(2 * j + 1) * CHUNK]
            pc = pb[:, (2 * j + 1) * CHUNK:(2 * j + 2) * CHUNK]
            ys = jnp.exp2(jnp.where(lane_lo, pa, pc)
                          + jnp.where(lane_lo_row, row(ROW_YS, 2 * j), row(ROW_YS, 2 * j + 1)))
            ma = (cb * jnp.exp2(jnp.where(mask, pa - row(ROW_Q, 2 * j), -jnp.inf))).astype(BF16)
            mc = (cb * jnp.exp2(jnp.where(mask, pc - row(ROW_Q, 2 * j + 1), -jnp.inf))).astype(BF16)
            tiles.append(y_off[:, 128 * j:128 * (j + 1)] * ys
                         + jnp.dot(ma, xp * xs_lo, preferred_element_type=F32)
                         + jnp.dot(mc, xp * xs_hi, preferred_element_type=F32))
        y = jnp.concatenate(tiles, axis=1)

        per_head = lambda off, dtype=F32: jnp.concatenate(
            [jnp.broadcast_to(row(off, r).astype(dtype), (SSM_HEAD_DIM, CHUNK)) for r in range(HEADS_PER_GROUP)],
            axis=0)
        xw = xt_ref[:, pl.ds(r0, CHUNK)] * per_head(ROW_W, BF16)
        new = jnp.dot(xw, bm, preferred_element_type=F32)
        st_ref[direction] = state_t * per_head(ROW_CD) + new
        return r0, xs, y

    def finish(r0, xs, y):
        y = y + y_ref[pl.ds(r0, CHUNK), :] + d_ref[...] * xs.astype(F32)
        y = y * _silu(z_ref[pl.ds(r0, CHUNK), :].astype(F32))
        ms = jnp.mean(y * y, axis=-1, keepdims=True)
        o_ref[pl.ds(r0, CHUNK), :] = (y * lax.rsqrt(ms + NORM_EPS) * nw_ref[...]).astype(o_ref.dtype)

    st_ref[...] = jnp.zeros_like(st_ref)

    def first_visits(i, carry):
        for c, direction in ((i, 0), (nchunks - 1 - i, 1)):
            r0, _, y = chunk(c, direction)
            y_ref[pl.ds(r0, CHUNK), :] = y
        return carry

    def second_visits(i, carry):
        for c, direction in ((i, 0), (nchunks - 1 - i, 1)):
            finish(*chunk(c, direction))
        return carry

    stage_ahead(0, 0)
    stage_ahead(nchunks - 1, 1)
    lax.fori_loop(0, half, first_visits, 0)
    lax.fori_loop(half, nchunks, second_visits, 0)


def _ssd(proj, dt_t, conv_w, conv_b, head_params, d_exp, norm_w, *, batch, seq):
    g = SSM_GROUPS
    width = conv_w.shape[0]
    assert width == CONV_TAPS and (seq // CHUNK) % 2 == 0
    sel = jnp.asarray(_selection_matrices(), BF16)
    shift = jnp.asarray(_shift_matrix(), BF16)
    x_blk = g
    b_blk = 2 * g * GROUP_X // SSM_STATE
    c_blk = b_blk + g
    cwb_blk = g * GROUP_X // SSM_STATE
    cwc_blk = cwb_blk + g
    return pl.pallas_call(
        _ssd_kernel,
        grid=(batch, g),
        in_specs=[
            pl.BlockSpec((seq, GROUP_X), lambda b, k: (b, k)),
            pl.BlockSpec((seq, GROUP_X), lambda b, k: (b, x_blk + k)),
            pl.BlockSpec((seq, SSM_STATE), lambda b, k: (b, b_blk + k)),
            pl.BlockSpec((seq, SSM_STATE), lambda b, k: (b, c_blk + k)),
            pl.BlockSpec((HEAD_ROWS, seq), lambda b, k: (k, b)),
            pl.BlockSpec((width, GROUP_X), lambda b, k: (0, k)),
            pl.BlockSpec((width, SSM_STATE), lambda b, k: (0, cwb_blk + k)),
            pl.BlockSpec((width, SSM_STATE), lambda b, k: (0, cwc_blk + k)),
            pl.BlockSpec((1, GROUP_X), lambda b, k: (0, k)),
            pl.BlockSpec((1, SSM_STATE), lambda b, k: (0, cwb_blk + k)),
            pl.BlockSpec((1, SSM_STATE), lambda b, k: (0, cwc_blk + k)),
            pl.BlockSpec((None, HEAD_ROWS, 4), lambda b, k: (k, 0, 0)),
            pl.BlockSpec((None, 1, GROUP_X), lambda b, k: (k, 0, 0)),
            pl.BlockSpec((None, 1, GROUP_X), lambda b, k: (k, 0, 0)),
            pl.BlockSpec(sel.shape, lambda b, k: (0, 0, 0)),
            pl.BlockSpec(shift.shape, lambda b, k: (0, 0)),
        ],
        out_specs=pl.BlockSpec((seq, GROUP_X), lambda b, k: (b, k)),
        out_shape=jax.ShapeDtypeStruct((batch * seq, g * GROUP_X), BF16),
        scratch_shapes=[
            pltpu.VMEM((seq, GROUP_XBC), BF16),
            pltpu.VMEM((GROUP_X, seq), BF16),
            pltpu.VMEM((seq, CHUNK), BF16),
            pltpu.VMEM((2 * ROW_TABLES, seq // CHUNK * HEAD_ROWS, CHUNK), F32),
            pltpu.VMEM((seq, GROUP_X), F32),
            pltpu.VMEM((2, GROUP_X, SSM_STATE), F32),
            pltpu.VMEM((2, CHUNK, SEL_COLS), F32),
            pltpu.VMEM((2, CHUNK, CHUNK), F32),
        ],
        compiler_params=_params(("parallel", "arbitrary")),
        name="ssd",
    )(proj, proj, proj, proj, dt_t, conv_w, conv_w, conv_w, conv_b, conv_b, conv_b,
      head_params, d_exp, norm_w, sel, shift)


def _out_proj_kernel(x_ref, a_ref, b_ref, wa_ref, wb_ref, o_ref):
    acc = jnp.dot(a_ref[...], wa_ref[...], preferred_element_type=F32)
    acc = acc + jnp.dot(b_ref[...], wb_ref[...], preferred_element_type=F32)
    o_ref[...] = x_ref[...] + acc


def _out_proj(x2, a_out, b_out, w_a, w_b, *, tm, tn):
    t, d = x2.shape
    ka, kb = a_out.shape[1], b_out.shape[1]
    return pl.pallas_call(
        _out_proj_kernel,
        grid=(t // tm, d // tn),
        in_specs=[
            pl.BlockSpec((tm, tn), lambda i, j: (i, j)),
            pl.BlockSpec((tm, ka), lambda i, j: (i, 0)),
            pl.BlockSpec((tm, kb), lambda i, j: (i, 0)),
            pl.BlockSpec((ka, tn), lambda i, j: (0, j)),
            pl.BlockSpec((kb, tn), lambda i, j: (0, j)),
        ],
        out_specs=pl.BlockSpec((tm, tn), lambda i, j: (i, j)),
        out_shape=jax.ShapeDtypeStruct((t, d), F32),
        compiler_params=_params(("parallel", "arbitrary")),
        name="out_proj",
    )(x2, a_out, b_out, w_a, w_b)


FFN_HALO = HALO


def _ffn_up_kernel(x_ref, xp_ref, xn_ref, g_ref, wg_ref, wv_ref, cwg_ref, cwv_ref, cbg_ref, cbv_ref,
                   o_ref, h_ref, *, tiles_per_seq):
    tm = x_ref.shape[0]
    i = pl.program_id(0)

    def norm(v):
        var = jnp.mean(v * v, axis=-1, keepdims=True)
        return v * lax.rsqrt(var + NORM_EPS) * g_ref[...]

    @pl.when(pl.program_id(1) == 0)
    def _():
        first = (i % tiles_per_seq) == 0
        last = (i % tiles_per_seq) == tiles_per_seq - 1
        h_ref[0:FFN_HALO, :] = jnp.where(first, 0.0, norm(xp_ref[...])).astype(BF16)
        h_ref[FFN_HALO + tm:, :] = jnp.where(last, 0.0, norm(xn_ref[...])).astype(BF16)

        def body(r, carry):
            r0 = pl.multiple_of(r * 128, 128)
            h_ref[pl.ds(FFN_HALO + r0, 128), :] = norm(x_ref[pl.ds(r0, 128), :]).astype(BF16)
            return carry
        lax.fori_loop(0, tm // 128, body, 0)

    rows = tm + 2 * FFN_HALO

    def conv(w_ref, cw_ref, cb_ref):
        up = jnp.dot(h_ref[...], w_ref[...], preferred_element_type=F32)
        acc = up[FFN_HALO:FFN_HALO + tm, :] * cw_ref[1:2, :] + cb_ref[...]
        acc = acc + pltpu.roll(up, 1, axis=0)[FFN_HALO:FFN_HALO + tm, :] * cw_ref[0:1, :]
        acc = acc + pltpu.roll(up, rows - 1, axis=0)[FFN_HALO:FFN_HALO + tm, :] * cw_ref[2:3, :]
        return acc

    gate = conv(wg_ref, cwg_ref, cbg_ref)
    val = conv(wv_ref, cwv_ref, cbv_ref)
    o_ref[...] = (_silu(gate) * val).astype(o_ref.dtype)


def _ffn_up(x1, gain, w_up, conv_w, conv_b, *, seq, tm, tn):
    t, d = x1.shape
    f = w_up.shape[1] // 2
    nj = f // tn
    tiles_per_seq = seq // tm
    hb = tm // FFN_HALO
    nhb = t // FFN_HALO
    return pl.pallas_call(
        functools.partial(_ffn_up_kernel, tiles_per_seq=tiles_per_seq),
        grid=(t // tm, nj),
        in_specs=[
            pl.BlockSpec((tm, d), lambda i, j: (i, 0)),
            pl.BlockSpec((FFN_HALO, d), lambda i, j: (jnp.maximum(i * hb - 1, 0), 0)),
            pl.BlockSpec((FFN_HALO, d), lambda i, j: (jnp.minimum((i + 1) * hb, nhb - 1), 0)),
            pl.BlockSpec((1, d), lambda i, j: (0, 0)),
            pl.BlockSpec((d, tn), lambda i, j: (0, j)),
            pl.BlockSpec((d, tn), lambda i, j: (0, j + nj)),
            pl.BlockSpec((3, tn), lambda i, j: (0, j)),
            pl.BlockSpec((3, tn), lambda i, j: (0, j + nj)),
            pl.BlockSpec((1, tn), lambda i, j: (0, j)),
            pl.BlockSpec((1, tn), lambda i, j: (0, j + nj)),
        ],
        out_specs=pl.BlockSpec((tm, tn), lambda i, j: (i, j)),
        out_shape=jax.ShapeDtypeStruct((t, f), BF16),
        scratch_shapes=[pltpu.VMEM((tm + 2 * FFN_HALO, d), BF16)],
        compiler_params=_params(("parallel", "arbitrary")),
        name="ffn_up",
    )(x1, x1, x1, gain, w_up, w_up, conv_w, conv_w, conv_b, conv_b)


def _ffn_down_kernel(x_ref, a_ref, w_ref, g_ref, o_ref, acc_ref):
    k = pl.program_id(1)

    @pl.when(k == 0)
    def _():
        acc_ref[...] = x_ref[...]

    acc_ref[...] += jnp.dot(a_ref[...], w_ref[...], preferred_element_type=F32)

    @pl.when(k == pl.num_programs(1) - 1)
    def _():
        v = acc_ref[...]
        var = jnp.mean(v * v, axis=-1, keepdims=True)
        o_ref[...] = v * lax.rsqrt(var + NORM_EPS) * g_ref[...]


def _ffn_down(x1, act, w_down, gain, *, tm, tk):
    t, d = x1.shape
    f = act.shape[1]
    return pl.pallas_call(
        _ffn_down_kernel,
        grid=(t // tm, f // tk),
        in_specs=[
            pl.BlockSpec((tm, d), lambda i, k: (i, 0)),
            pl.BlockSpec((tm, tk), lambda i, k: (i, k)),
            pl.BlockSpec((tk, d), lambda i, k: (k, 0)),
            pl.BlockSpec((1, d), lambda i, k: (0, 0)),
        ],
        out_specs=pl.BlockSpec((tm, d), lambda i, k: (i, 0)),
        out_shape=jax.ShapeDtypeStruct((t, d), F32),
        scratch_shapes=[pltpu.VMEM((tm, d), F32)],
        compiler_params=_params(("parallel", "arbitrary")),
        name="ffn_down",
    )(x1, act, w_down, gain)


def _layer(x2, batch, seq, norm_mix_w, w_in, fourier_w, ssm_conv_w, ssm_conv_b, dt_bias_fwd, a_log_fwd,
           dt_bias_bwd, a_log_bwd, ssm_d, ssm_norm_w, w_out, norm_ffn_w, w_up, ffn_conv_w, ffn_conv_b,
           w_down, final_gain, tiles):
    d = x2.shape[1]
    g, r = SSM_GROUPS, HEADS_PER_GROUP
    fw = fourier_w.shape[0] * fourier_w.shape[1]
    sw = g * GROUP_X
    gn = g * SSM_STATE
    main = fw + 2 * sw + 2 * gn

    wdt = jnp.pad(w_in[:, main:].reshape(d, g, r), ((0, 0), (0, 0), (0, HEAD_ROWS - r)))
    wdt = jnp.pad(wdt.reshape(d, g * HEAD_ROWS), ((0, 0), (0, 128 - g * HEAD_ROWS))).astype(BF16)

    proj, dt_t = _in_proj(x2, norm_mix_w[None, :], w_in.astype(BF16), wdt, n=main, nh=g * HEAD_ROWS,
                          tm=tiles["in_tm"], tn=tiles["in_tn"])

    a_out = _fourier_fft(proj, fourier_w, batch=batch, seq=seq, col0=main - fw)

    hp = jnp.stack([dt_bias_fwd, a_log_fwd, dt_bias_bwd, a_log_bwd], axis=-1).reshape(g, r, 4)
    hp = jnp.pad(hp, ((0, 0), (0, HEAD_ROWS - r), (0, 0)))
    d_exp = jnp.repeat(ssm_d, SSM_HEAD_DIM).reshape(g, 1, GROUP_X)
    b_out = _ssd(proj, dt_t, ssm_conv_w, ssm_conv_b[None, :], hp, d_exp, ssm_norm_w.reshape(g, 1, GROUP_X),
                 batch=batch, seq=seq)

    w_out_b = w_out.astype(BF16)
    x1 = _out_proj(x2, a_out, b_out, w_out_b[:fw], w_out_b[fw:], tm=tiles["o_tm"], tn=tiles["o_tn"])

    act = _ffn_up(x1, norm_ffn_w[None, :], w_up.astype(BF16), ffn_conv_w, ffn_conv_b[None, :],
                  seq=seq, tm=tiles["u_tm"], tn=tiles["u_tn"])
    return _ffn_down(x1, act, w_down.astype(BF16), final_gain[None, :], tm=tiles["d_tm"], tk=tiles["d_tk"])


TILES = dict(in_tm=1024, in_tn=1024, f_tm=512, o_tm=1024, o_tn=512, u_tm=1024, u_tn=512, d_tm=1024, d_tk=512)


def kernel(x, norm_mix_w, w_in, fourier_w, ssm_conv_w, ssm_conv_b, dt_bias_fwd, a_log_fwd, dt_bias_bwd,
           a_log_bwd, ssm_d, ssm_norm_w, w_out, norm_ffn_w, w_up, ffn_conv_w, ffn_conv_b, w_down,
           norm_final_w):
    batch, seq, d = x.shape
    assert norm_mix_w.shape[0] == 1, "one layer"
    out = _layer(x.reshape(batch * seq, d), batch, seq, norm_mix_w[0], w_in[0], fourier_w[0], ssm_conv_w[0],
                 ssm_conv_b[0], dt_bias_fwd[0], a_log_fwd[0], dt_bias_bwd[0], a_log_bwd[0], ssm_d[0],
                 ssm_norm_w[0], w_out[0], norm_ffn_w[0], w_up[0], ffn_conv_w[0], ffn_conv_b[0], w_down[0],
                 norm_final_w, TILES)
    return out.reshape(batch, seq, d)
```

```python
import functools

import jax
import jax.numpy as jnp
import numpy as np
from jax import lax
from jax.experimental import pallas as pl
from jax.experimental.pallas import tpu as pltpu

NORM_EPS = 1e-5
FOURIER_GROUPS = 8
SSM_GROUPS = 8
HEADS_PER_GROUP = 6
HEAD_ROWS = 8
SSM_HEAD_DIM = 64
SSM_STATE = 128
CHUNK = 128
GROUP_X = HEADS_PER_GROUP * SSM_HEAD_DIM
GROUP_XBC = GROUP_X + 2 * SSM_STATE
GROUP_COLS = GROUP_X + GROUP_XBC
HALO = 16
VMEM_LIMIT = 56 * 1024 * 1024

F32 = jnp.float32
BF16 = jnp.bfloat16


def _params(semantics):
    return pltpu.CompilerParams(dimension_semantics=semantics, vmem_limit_bytes=VMEM_LIMIT)


def _silu(v):
    return v / (1.0 + jnp.exp(-v))


def _in_proj_kernel(x_ref, g_ref, w_ref, wdt_ref, o_ref, dt_ref, h_ref, *, sub):
    @pl.when(pl.program_id(1) == 0)
    def _():
        def body(r, carry):
            r0 = pl.multiple_of(r * sub, sub)
            xs = x_ref[pl.ds(r0, sub), :]
            var = jnp.mean(xs * xs, axis=-1, keepdims=True)
            hs = xs * lax.rsqrt(var + NORM_EPS) * g_ref[...]
            h_ref[pl.ds(r0, sub), :] = hs.astype(BF16)
            return carry
        lax.fori_loop(0, x_ref.shape[0] // sub, body, 0)
        dt = jnp.dot(h_ref[...], wdt_ref[...], preferred_element_type=F32)
        lanes = wdt_ref.shape[1]
        for r in range(x_ref.shape[0] // lanes):
            dt_ref[:, r * lanes:(r + 1) * lanes] = jnp.transpose(
                dt[r * lanes:(r + 1) * lanes, :])[:dt_ref.shape[0], :]

    o_ref[...] = jnp.dot(h_ref[...], w_ref[...], preferred_element_type=F32).astype(o_ref.dtype)


def _in_proj(x2, gain, w, wdt, *, n, nh, tm, tn):
    t, d = x2.shape
    nj = n // tn
    return pl.pallas_call(
        functools.partial(_in_proj_kernel, sub=128),
        grid=(t // tm, nj),
        in_specs=[
            pl.BlockSpec((tm, d), lambda i, j: (i, 0)),
            pl.BlockSpec((1, d), lambda i, j: (0, 0)),
            pl.BlockSpec((d, tn), lambda i, j: (0, (j + 1) % nj)),
            pl.BlockSpec(wdt.shape, lambda i, j: (0, 0)),
        ],
        out_specs=[
            pl.BlockSpec((tm, tn), lambda i, j: (i, j)),
            pl.BlockSpec((nh, tm), lambda i, j: (0, i)),
        ],
        out_shape=[jax.ShapeDtypeStruct((t, n), BF16), jax.ShapeDtypeStruct((nh, t), F32)],
        scratch_shapes=[pltpu.VMEM((tm, d), BF16)],
        compiler_params=_params(("parallel", "arbitrary")),
        name="in_proj",
    )(x2, gain, w, wdt)


def _fourier_kernel(cos_ref, sin_ref, u_ref, cc_ref, sc_ref, w_ref, o_ref, *, scale):
    u = u_ref[...]
    uc = jnp.dot(cos_ref[...], u, preferred_element_type=F32)
    us = jnp.dot(sin_ref[...], u, preferred_element_type=F32)
    c = cc_ref.shape[0]
    for g in range(w_ref.shape[0]):
        ucg = uc[:, g * c:(g + 1) * c].astype(BF16)
        usg = us[:, g * c:(g + 1) * c].astype(BF16)
        re = (jnp.dot(ucg, cc_ref[...], preferred_element_type=F32)
              - jnp.dot(usg, sc_ref[...], preferred_element_type=F32)) * scale
        o_ref[:, g * c:(g + 1) * c] = jnp.dot(
            re.astype(BF16), w_ref[g], preferred_element_type=F32).astype(o_ref.dtype)


def _fourier(cos_m, sin_m, proj, cc, sc, w_mix, *, batch, seq, tm):
    g, c, _ = w_mix.shape
    width = g * c
    scale = float(1.0 / np.sqrt(seq * c))
    nt = seq // tm
    return pl.pallas_call(
        functools.partial(_fourier_kernel, scale=scale),
        grid=(batch, nt),
        in_specs=[
            pl.BlockSpec((tm, seq), lambda b, i: (i, 0)),
            pl.BlockSpec((tm, seq), lambda b, i: (i, 0)),
            pl.BlockSpec((seq, width), lambda b, i: (b, 0)),
            pl.BlockSpec((c, c), lambda b, i: (0, 0)),
            pl.BlockSpec((c, c), lambda b, i: (0, 0)),
            pl.BlockSpec((g, c, c), lambda b, i: (0, 0, 0)),
        ],
        out_specs=pl.BlockSpec((tm, width), lambda b, i: (b * nt + i, 0)),
        out_shape=jax.ShapeDtypeStruct((batch * seq, width), BF16),
        compiler_params=_params(("parallel", "arbitrary")),
        name="fourier",
    )(cos_m, sin_m, proj, cc, sc, w_mix)


FFT_SPLIT = 4
FFT_COLS = 256


def _fourier_fft_kernel(u_ref, cc_ref, sc_ref, w_ref, l2_ref, twc_ref, tws_ref, o_ref, z_ref, o32_ref, *, scale):
    n2 = l2_ref.shape[0]
    c = cc_ref.shape[0]
    groups = FFT_COLS // c
    cb = pl.program_id(1)
    for gi in range(groups):
        w = w_ref[cb * groups + gi]
        mix_c = (jnp.dot(cc_ref[...], w, preferred_element_type=F32) * scale).astype(BF16)
        mix_s = (jnp.dot(sc_ref[...], w, preferred_element_type=F32) * scale).astype(BF16)
        ug = u_ref[:, gi * c:(gi + 1) * c]
        z_ref[0, :, gi * c:(gi + 1) * c] = jnp.dot(ug, mix_c, preferred_element_type=F32)
        z_ref[1, :, gi * c:(gi + 1) * c] = -jnp.dot(ug, mix_s, preferred_element_type=F32)

    zc = [z_ref[0, a * n2:(a + 1) * n2, :] for a in range(FFT_SPLIT)]
    zs = [z_ref[1, a * n2:(a + 1) * n2, :] for a in range(FFT_SPLIT)]
    ec, es, fc, fs = zc[0] + zc[2], zs[0] + zs[2], zc[0] - zc[2], zs[0] - zs[2]
    gc, gs, hc, hs = zc[1] + zc[3], zs[1] + zs[3], zc[1] - zc[3], zs[1] - zs[3]
    butterflies = [(ec + gc, es + gs), (fc + hs, fs - hc), (ec - gc, es - gs), (fc - hs, fs + hc)]
    lane_reps = FFT_COLS // twc_ref.shape[2]
    for k_lo, (yc, ys) in enumerate(butterflies):
        if k_lo > 0:
            twc = jnp.concatenate([twc_ref[k_lo - 1]] * lane_reps, axis=1)
            tws = jnp.concatenate([tws_ref[k_lo - 1]] * lane_reps, axis=1)
            yc, ys = yc * twc + ys * tws, ys * twc - yc * tws
        rhs = jnp.concatenate([yc, ys], axis=0).astype(BF16)
        x = jnp.dot(l2_ref[...], rhs, preferred_element_type=F32)
        for h in range(FFT_COLS // 128):
            o32_ref[h, pl.ds(k_lo, n2, stride=FFT_SPLIT), :] = x[:, 128 * h:128 * (h + 1)]
    for h in range(FFT_COLS // 128):
        o_ref[:, 128 * h:128 * (h + 1)] = o32_ref[h].astype(o_ref.dtype)


def _fourier_fft(proj, w_mix, *, batch, seq, col0):
    g, c, _ = w_mix.shape
    n2 = seq // FFT_SPLIT
    assert seq % FFT_SPLIT == 0 and FFT_COLS % c == 0 and (g * c) % FFT_COLS == 0 and col0 % FFT_COLS == 0
    blk0 = col0 // FFT_COLS
    scale = float(1.0 / np.sqrt(seq * c))
    cos_c, sin_c = _dft_tables(c, 8)
    cos_n, sin_n = _dft_tables(n2, 32)
    l2 = jnp.concatenate([cos_n, sin_n], axis=1).astype(BF16)
    k_lo = jnp.arange(1, FFT_SPLIT, dtype=jnp.int32)[:, None]
    ang = (k_lo * jnp.arange(n2, dtype=jnp.int32)[None, :]).astype(F32) * (2.0 * np.pi / seq)
    twc = jnp.broadcast_to(jnp.cos(ang)[:, :, None], (FFT_SPLIT - 1, n2, 128))
    tws = jnp.broadcast_to(jnp.sin(ang)[:, :, None], (FFT_SPLIT - 1, n2, 128))
    const2 = lambda b, j: (0, 0)
    const3 = lambda b, j: (0, 0, 0)
    return pl.pallas_call(
        functools.partial(_fourier_fft_kernel, scale=scale),
        grid=(batch, g * c // FFT_COLS),
        in_specs=[
            pl.BlockSpec((seq, FFT_COLS), lambda b, j: (b, blk0 + j)),
            pl.BlockSpec((c, c), const2),
            pl.BlockSpec((c, c), const2),
            pl.BlockSpec((g, c, c), const3),
            pl.BlockSpec(l2.shape, const2),
            pl.BlockSpec(twc.shape, const3),
            pl.BlockSpec(tws.shape, const3),
        ],
        out_specs=pl.BlockSpec((seq, FFT_COLS), lambda b, j: (b, j)),
        out_shape=jax.ShapeDtypeStruct((batch * seq, g * c), BF16),
        scratch_shapes=[pltpu.VMEM((2, seq, FFT_COLS), F32),
                        pltpu.VMEM((FFT_COLS // 128, seq, 128), F32)],
        compiler_params=_params(("parallel", "arbitrary")),
        name="fourier",
    )(proj, cos_c.astype(BF16), sin_c.astype(BF16), w_mix.astype(BF16), l2, twc, tws)


def _dft_tables(n, split):
    hi = n // split
    s = jnp.arange(n, dtype=jnp.int32)[None, :]
    k1 = jnp.arange(hi, dtype=jnp.int32)[:, None]
    k0 = jnp.arange(split, dtype=jnp.int32)[:, None]
    a1 = ((k1 * s * split) % n).astype(F32) * (2.0 * np.pi / n)
    a0 = ((k0 * s) % n).astype(F32) * (2.0 * np.pi / n)
    c1, s1 = jnp.cos(a1)[:, None, :], jnp.sin(a1)[:, None, :]
    c0, s0 = jnp.cos(a0)[None, :, :], jnp.sin(a0)[None, :, :]
    cos_m = (c1 * c0 - s1 * s0).reshape(n, n)
    sin_m = (s1 * c0 + c1 * s0).reshape(n, n)
    return cos_m, sin_m


PIECES_P = 3
COL_DIR = PIECES_P * HEAD_ROWS
SEL_COLS = HEADS_PER_GROUP * CHUNK
ROW_Q, ROW_YS, ROW_W, ROW_CD = 0, 1, 2, 3
ROW_TABLES = 4
LOG2E = float(np.log2(np.e))
CONV_TAPS = 5
WIN_ROWS = CHUNK + 2 * HALO


def _selection_matrices():
    sel = np.zeros((2, CHUNK, SEL_COLS), np.float32)
    for d in range(2):
        for r in range(HEADS_PER_GROUP):
            for piece in range(PIECES_P):
                sel[d, d * COL_DIR + piece * HEAD_ROWS + r, r * CHUNK:(r + 1) * CHUNK] = 1.0
    return sel


def _shift_matrix():
    taps = [k for k in range(CONV_TAPS) if k != CONV_TAPS // 2]
    m = np.zeros((len(taps) * CHUNK, WIN_ROWS), np.float32)
    for i, k in enumerate(taps):
        for t in range(CHUNK):
            m[i * CHUNK + t, HALO + t + k - CONV_TAPS // 2] = 1.0
    return m


def _softplus(v):
    return jnp.maximum(v, 0.0) + jnp.log1p(jnp.exp(-jnp.abs(v)))


def _bf16_pieces(v, n):
    out = []
    for _ in range(n - 1):
        hi = v.astype(BF16).astype(F32)
        out.append(hi)
        v = v - hi
    out.append(v)
    return out


def _ssd_kernel(z_ref, x_ref, b_ref, c_ref, dt_ref, cwx_ref, cwb_ref, cwc_ref, cbx_ref, cbb_ref, cbc_ref,
                hp_ref, d_ref, nw_ref, sel_ref, shift_ref, o_ref,
                xc_ref, xt_ref, colq_ref, rowq_ref, y_ref, st_ref, pb_ref, cbm_ref):
    seq = x_ref.shape[0]
    nchunks = seq // CHUNK
    half = nchunks // 2
    xbc_rows = lambda start, n: jnp.concatenate(
        [ref[pl.ds(start, n), :] for ref in (x_ref, b_ref, c_ref)], axis=1)
    conv_w = jnp.concatenate([cwx_ref[...], cwb_ref[...], cwc_ref[...]], axis=1)
    conv_b = jnp.concatenate([cbx_ref[...], cbb_ref[...], cbc_ref[...]], axis=1)
    li = lax.broadcasted_iota(jnp.int32, (CHUNK, CHUNK), 0)
    si = lax.broadcasted_iota(jnp.int32, (CHUNK, CHUNK), 1)
    upper_b = jnp.where(li <= si, 1.0, 0.0).astype(BF16)
    lane_lo = si < SSM_HEAD_DIM
    lane_lo_row = lax.broadcasted_iota(jnp.int32, (1, 128), 1) < SSM_HEAD_DIM

    hp = hp_ref[...]
    bias_f, a_f = hp[:, 0:1], -jnp.exp(hp[:, 1:2])
    bias_b, a_b = hp[:, 2:3], -jnp.exp(hp[:, 3:4])

    def prep(c):
        r0 = pl.multiple_of(c * CHUNK, CHUNK)
        cur = xbc_rows(r0, CHUNK)
        p0 = pl.multiple_of(jnp.maximum(r0 - HALO, 0), HALO)
        n0 = pl.multiple_of(jnp.minimum(r0 + CHUNK, seq - HALO), HALO)
        prev = xbc_rows(p0, HALO)
        nxt = xbc_rows(n0, HALO)
        prev = jnp.where(c > 0, prev, jnp.zeros_like(prev))
        nxt = jnp.where(c < nchunks - 1, nxt, jnp.zeros_like(nxt))
        win = jnp.concatenate([prev, cur, nxt], axis=0)
        taps = jnp.dot(shift_ref[...], win, preferred_element_type=F32)
        mid = CONV_TAPS // 2
        acc = cur.astype(F32) * conv_w[mid:mid + 1, :] + conv_b
        for i, k in enumerate([k for k in range(CONV_TAPS) if k != mid]):
            acc = acc + taps[i * CHUNK:(i + 1) * CHUNK, :] * conv_w[k:k + 1, :]
        act = _silu(acc)
        xc_ref[pl.ds(r0, CHUNK), :] = act.astype(BF16)
        xt_ref[:, pl.ds(r0, CHUNK)] = jnp.concatenate(
            [jnp.transpose(act[:, 128 * j:128 * (j + 1)]) for j in range(GROUP_X // 128)], axis=0).astype(BF16)

    prep_unroll = min(8, nchunks)

    def prep_group(i, carry):
        for u in range(prep_unroll):
            prep(prep_unroll * i + u)
        return carry

    lax.fori_loop(0, nchunks // prep_unroll, prep_group, 0)

    trows = nchunks * HEAD_ROWS
    dt_all = dt_ref[...]
    dtr = jnp.concatenate([dt_all[:, c * CHUNK:(c + 1) * CHUNK] for c in range(nchunks)], axis=0)
    per_row = lambda col: jnp.concatenate([jnp.broadcast_to(col, (HEAD_ROWS, CHUNK))] * nchunks, axis=0)
    lane_const = lambda v: jnp.broadcast_to(v, (trows, CHUNK))

    def cumsum(v):
        stacked = jnp.concatenate(_bf16_pieces(v, 3), axis=0).astype(BF16)
        out = jnp.dot(stacked, upper_b, preferred_element_type=F32)
        return (out[:trows] + out[trows:2 * trows]) + out[2 * trows:]

    dt_f = _softplus(dtr + per_row(bias_f))
    dt_b = _softplus(dtr + per_row(bias_b))
    adt_f = dt_f * per_row(a_f)
    adt_b = dt_b * per_row(a_b)
    cum_f = cumsum(adt_f)
    cum_b = cumsum(adt_b)
    tot_f = cum_f[:, CHUNK - 1:CHUNK]
    tot_b = cum_b[:, CHUNK - 1:CHUNK]
    ex_b = cum_b - adt_b
    p2_f = cum_f * LOG2E
    p2_b = -ex_b * LOG2E
    tables = [p2_f - jnp.log(dt_f) * LOG2E, jnp.zeros((trows, CHUNK), F32),
              jnp.exp(tot_f - cum_f) * dt_f, lane_const(jnp.exp(tot_f)),
              p2_b - jnp.log(dt_b) * LOG2E, lane_const(tot_b * LOG2E),
              jnp.exp(ex_b) * dt_b, lane_const(jnp.exp(tot_b))]
    for q, tab in enumerate(tables):
        rowq_ref[q] = tab
    pieces = _bf16_pieces(p2_f, PIECES_P) + _bf16_pieces(p2_b, PIECES_P)
    pad = jnp.zeros((CHUNK - 2 * COL_DIR, CHUNK), F32)
    for c in range(nchunks):
        rows = [p[c * HEAD_ROWS:(c + 1) * HEAD_ROWS, :] for p in pieces] + [pad]
        colq_ref[c * CHUNK:(c + 1) * CHUNK, :] = jnp.transpose(jnp.concatenate(rows, axis=0)).astype(BF16)

    nt = (((1,), (1,)), ((), ()))

    def stage_ahead(c, direction):
        r0 = pl.multiple_of(c * CHUNK, CHUNK)
        pb_ref[direction] = jnp.dot(colq_ref[pl.ds(r0, CHUNK), :], sel_ref[direction],
                                    preferred_element_type=F32)
        cbm_ref[direction] = lax.dot_general(
            xc_ref[pl.ds(r0, CHUNK), GROUP_X + SSM_STATE:], xc_ref[pl.ds(r0, CHUNK), GROUP_X:GROUP_X + SSM_STATE],
            nt, preferred_element_type=F32)

    def chunk(c, direction):
        r0 = pl.multiple_of(c * CHUNK, CHUNK)
        xcb = xc_ref[pl.ds(r0, CHUNK), :]
        xs = xcb[:, :GROUP_X]
        bm = xcb[:, GROUP_X:GROUP_X + SSM_STATE]
        cm = xcb[:, GROUP_X + SSM_STATE:]
        r8 = pl.multiple_of(c * HEAD_ROWS, HEAD_ROWS)
        tabs = [rowq_ref[direction * ROW_TABLES + q, pl.ds(r8, HEAD_ROWS), :] for q in range(ROW_TABLES)]
        row = lambda q, r: tabs[q][r:r + 1, :]
        pb = pb_ref[direction]
        cb = cbm_ref[direction]
        step = 1 if direction == 0 else -1
        stage_ahead(jnp.clip(c + step, 0, nchunks - 1), direction)
        mask = (li >= si) if direction == 0 else (si >= li)

        state_t = st_ref[direction]
        y_off = lax.dot_general(cm, state_t.astype(BF16), nt, preferred_element_type=F32)

        xs_lo = jnp.where(lane_lo_row, 1.0, 0.0).astype(BF16)
        xs_hi = jnp.where(lane_lo_row, 0.0, 1.0).astype(BF16)
        tiles = []
        for j in range(HEADS_PER_GROUP // 2):
            xp = xs[:, 128 * j:128 * (j + 1)]
            pa = pb[:, 2 * j * CHUNK:(2 * j + 1) * CHUNK]
            pc = pb[:, (2 * j + 1) * CHUNK:(2 * j + 2) * CHUNK]
            ys = jnp.exp2(jnp.where(lane_lo, pa, pc)
                          + jnp.where(lane_lo_row, row(ROW_YS, 2 * j), row(ROW_YS, 2 * j + 1)))
            ma = (cb * jnp.exp2(jnp.where(mask, pa - row(ROW_Q, 2 * j), -jnp.inf))).astype(BF16)
            mc = (cb * jnp.exp2(jnp.where(mask, pc - row(ROW_Q, 2 * j + 1), -jnp.inf))).astype(BF16)
            tiles.append(y_off[:, 128 * j:128 * (j + 1)] * ys
                         + jnp.dot(ma, xp * xs_lo, preferred_element_type=F32)
                         + jnp.dot(mc, xp * xs_hi, preferred_element_type=F32))
        y = jnp.concatenate(tiles, axis=1)

        per_head = lambda off, dtype=F32: jnp.concatenate(
            [jnp.broadcast_to(row(off, r).astype(dtype), (SSM_HEAD_DIM, CHUNK)) for r in range(HEADS_PER_GROUP)],
            axis=0)
        xw = xt_ref[:, pl.ds(r0, CHUNK)] * per_head(ROW_W, BF16)
        new = jnp.dot(xw, bm, preferred_element_type=F32)
        st_ref[direction] = state_t * per_head(ROW_CD) + new
        return r0, xs, y

    def finish(r0, xs, y):
        y = y + y_ref[pl.ds(r0, CHUNK), :] + d_ref[...] * xs.astype(F32)
        y = y * _silu(z_ref[pl.ds(r0, CHUNK), :].astype(F32))
        ms = jnp.mean(y * y, axis=-1, keepdims=True)
        o_ref[pl.ds(r0, CHUNK), :] = (y * lax.rsqrt(ms + NORM_EPS) * nw_ref[...]).astype(o_ref.dtype)

    st_ref[...] = jnp.zeros_like(st_ref)

    scan_unroll = 4 if half % 4 == 0 else (2 if half % 2 == 0 else 1)

    def first_visits(t, carry):
        for u in range(scan_unroll):
            i = scan_unroll * t + u
            for c, direction in ((i, 0), (nchunks - 1 - i, 1)):
                r0, _, y = chunk(c, direction)
                y_ref[pl.ds(r0, CHUNK), :] = y
        return carry

    def second_visits(t, carry):
        for u in range(scan_unroll):
            i = half + scan_unroll * t + u
            for c, direction in ((i, 0), (nchunks - 1 - i, 1)):
                finish(*chunk(c, direction))
        return carry

    stage_ahead(0, 0)
    stage_ahead(nchunks - 1, 1)
    lax.fori_loop(0, half // scan_unroll, first_visits, 0)
    lax.fori_loop(0, half // scan_unroll, second_visits, 0)


def _ssd(proj, dt_t, conv_w, conv_b, head_params, d_exp, norm_w, *, batch, seq):
    g = SSM_GROUPS
    width = conv_w.shape[0]
    assert width == CONV_TAPS and (seq // CHUNK) % 2 == 0
    sel = jnp.asarray(_selection_matrices(), BF16)
    shift = jnp.asarray(_shift_matrix(), BF16)
    x_blk = g
    b_blk = 2 * g * GROUP_X // SSM_STATE
    c_blk = b_blk + g
    cwb_blk = g * GROUP_X // SSM_STATE
    cwc_blk = cwb_blk + g
    return pl.pallas_call(
        _ssd_kernel,
        grid=(batch, g),
        in_specs=[
            pl.BlockSpec((seq, GROUP_X), lambda b, k: (b, k)),
            pl.BlockSpec((seq, GROUP_X), lambda b, k: (b, x_blk + k)),
            pl.BlockSpec((seq, SSM_STATE), lambda b, k: (b, b_blk + k)),
            pl.BlockSpec((seq, SSM_STATE), lambda b, k: (b, c_blk + k)),
            pl.BlockSpec((HEAD_ROWS, seq), lambda b, k: (k, b)),
            pl.BlockSpec((width, GROUP_X), lambda b, k: (0, k)),
            pl.BlockSpec((width, SSM_STATE), lambda b, k: (0, cwb_blk + k)),
            pl.BlockSpec((width, SSM_STATE), lambda b, k: (0, cwc_blk + k)),
            pl.BlockSpec((1, GROUP_X), lambda b, k: (0, k)),
            pl.BlockSpec((1, SSM_STATE), lambda b, k: (0, cwb_blk + k)),
            pl.BlockSpec((1, SSM_STATE), lambda b, k: (0, cwc_blk + k)),
            pl.BlockSpec((None, HEAD_ROWS, 4), lambda b, k: (k, 0, 0)),
            pl.BlockSpec((None, 1, GROUP_X), lambda b, k: (k, 0, 0)),
            pl.BlockSpec((None, 1, GROUP_X), lambda b, k: (k, 0, 0)),
            pl.BlockSpec(sel.shape, lambda b, k: (0, 0, 0)),
            pl.BlockSpec(shift.shape, lambda b, k: (0, 0)),
        ],
        out_specs=pl.BlockSpec((seq, GROUP_X), lambda b, k: (b, k)),
        out_shape=jax.ShapeDtypeStruct((batch * seq, g * GROUP_X), BF16),
        scratch_shapes=[
            pltpu.VMEM((seq, GROUP_XBC), BF16),
            pltpu.VMEM((GROUP_X, seq), BF16),
            pltpu.VMEM((seq, CHUNK), BF16),
            pltpu.VMEM((2 * ROW_TABLES, seq // CHUNK * HEAD_ROWS, CHUNK), F32),
            pltpu.VMEM((seq, GROUP_X), F32),
            pltpu.VMEM((2, GROUP_X, SSM_STATE), F32),
            pltpu.VMEM((2, CHUNK, SEL_COLS), F32),
            pltpu.VMEM((2, CHUNK, CHUNK), F32),
        ],
        compiler_params=_params(("parallel", "arbitrary")),
        name="ssd",
    )(proj, proj, proj, proj, dt_t, conv_w, conv_w, conv_w, conv_b, conv_b, conv_b,
      head_params, d_exp, norm_w, sel, shift)


def _out_proj_kernel(x_ref, a_ref, b_ref, wa_ref, wb_ref, o_ref):
    acc = jnp.dot(a_ref[...], wa_ref[...], preferred_element_type=F32)
    acc = acc + jnp.dot(b_ref[...], wb_ref[...], preferred_element_type=F32)
    o_ref[...] = x_ref[...] + acc


def _out_proj(x2, a_out, b_out, w_a, w_b, *, tm, tn):
    t, d = x2.shape
    ka, kb = a_out.shape[1], b_out.shape[1]
    return pl.pallas_call(
        _out_proj_kernel,
        grid=(t // tm, d // tn),
        in_specs=[
            pl.BlockSpec((tm, tn), lambda i, j: (i, j)),
            pl.BlockSpec((tm, ka), lambda i, j: (i, 0)),
            pl.BlockSpec((tm, kb), lambda i, j: (i, 0)),
            pl.BlockSpec((ka, tn), lambda i, j: (0, j)),
            pl.BlockSpec((kb, tn), lambda i, j: (0, j)),
        ],
        out_specs=pl.BlockSpec((tm, tn), lambda i, j: (i, j)),
        out_shape=jax.ShapeDtypeStruct((t, d), F32),
        compiler_params=_params(("parallel", "arbitrary")),
        name="out_proj",
    )(x2, a_out, b_out, w_a, w_b)


FFN_HALO = HALO


def _ffn_up_kernel(x_ref, xp_ref, xn_ref, g_ref, wg_ref, wv_ref, cwg_ref, cwv_ref, cbg_ref, cbv_ref,
                   o_ref, h_ref, *, tiles_per_seq):
    tm = x_ref.shape[0]
    i = pl.program_id(0)

    def norm(v):
        var = jnp.mean(v * v, axis=-1, keepdims=True)
        return v * lax.rsqrt(var + NORM_EPS) * g_ref[...]

    @pl.when(pl.program_id(1) == 0)
    def _():
        first = (i % tiles_per_seq) == 0
        last = (i % tiles_per_seq) == tiles_per_seq - 1
        h_ref[0:FFN_HALO, :] = jnp.where(first, 0.0, norm(xp_ref[...])).astype(BF16)
        h_ref[FFN_HALO + tm:, :] = jnp.where(last, 0.0, norm(xn_ref[...])).astype(BF16)

        def body(r, carry):
            r0 = pl.multiple_of(r * 128, 128)
            h_ref[pl.ds(FFN_HALO + r0, 128), :] = norm(x_ref[pl.ds(r0, 128), :]).astype(BF16)
            return carry
        lax.fori_loop(0, tm // 128, body, 0)

    rows = tm + 2 * FFN_HALO

    def conv(w_ref, cw_ref, cb_ref):
        up = jnp.dot(h_ref[...], w_ref[...], preferred_element_type=F32)
        acc = up[FFN_HALO:FFN_HALO + tm, :] * cw_ref[1:2, :] + cb_ref[...]
        acc = acc + pltpu.roll(up, 1, axis=0)[FFN_HALO:FFN_HALO + tm, :] * cw_ref[0:1, :]
        acc = acc + pltpu.roll(up, rows - 1, axis=0)[FFN_HALO:FFN_HALO + tm, :] * cw_ref[2:3, :]
        return acc

    gate = conv(wg_ref, cwg_ref, cbg_ref)
    val = conv(wv_ref, cwv_ref, cbv_ref)
    o_ref[...] = (_silu(gate) * val).astype(o_ref.dtype)


def _ffn_up(x1, gain, w_up, conv_w, conv_b, *, seq, tm, tn):
    t, d = x1.shape
    f = w_up.shape[1] // 2
    nj = f // tn
    tiles_per_seq = seq // tm
    hb = tm // FFN_HALO
    nhb = t // FFN_HALO
    return pl.pallas_call(
        functools.partial(_ffn_up_kernel, tiles_per_seq=tiles_per_seq),
        grid=(t // tm, nj),
        in_specs=[
            pl.BlockSpec((tm, d), lambda i, j: (i, 0)),
            pl.BlockSpec((FFN_HALO, d), lambda i, j: (jnp.maximum(i * hb - 1, 0), 0)),
            pl.BlockSpec((FFN_HALO, d), lambda i, j: (jnp.minimum((i + 1) * hb, nhb - 1), 0)),
            pl.BlockSpec((1, d), lambda i, j: (0, 0)),
            pl.BlockSpec((d, tn), lambda i, j: (0, j)),
            pl.BlockSpec((d, tn), lambda i, j: (0, j + nj)),
            pl.BlockSpec((3, tn), lambda i, j: (0, j)),
            pl.BlockSpec((3, tn), lambda i, j: (0, j + nj)),
            pl.BlockSpec((1, tn), lambda i, j: (0, j)),
            pl.BlockSpec((1, tn), lambda i, j: (0, j + nj)),
        ],
        out_specs=pl.BlockSpec((tm, tn), lambda i, j: (i, j)),
        out_shape=jax.ShapeDtypeStruct((t, f), BF16),
        scratch_shapes=[pltpu.VMEM((tm + 2 * FFN_HALO, d), BF16)],
        compiler_params=_params(("parallel", "arbitrary")),
        name="ffn_up",
    )(x1, x1, x1, gain, w_up, w_up, conv_w, conv_w, conv_b, conv_b)


def _ffn_down_kernel(x_ref, a_ref, w_ref, g_ref, o_ref, acc_ref):
    k = pl.program_id(1)

    @pl.when(k == 0)
    def _():
        acc_ref[...] = x_ref[...]

    acc_ref[...] += jnp.dot(a_ref[...], w_ref[...], preferred_element_type=F32)

    @pl.when(k == pl.num_programs(1) - 1)
    def _():
        v = acc_ref[...]
        var = jnp.mean(v * v, axis=-1, keepdims=True)
        o_ref[...] = v * lax.rsqrt(var + NORM_EPS) * g_ref[...]


def _ffn_down(x1, act, w_down, gain, *, tm, tk):
    t, d = x1.shape
    f = act.shape[1]
    return pl.pallas_call(
        _ffn_down_kernel,
        grid=(t // tm, f // tk),
        in_specs=[
            pl.BlockSpec((tm, d), lambda i, k: (i, 0)),
            pl.BlockSpec((tm, tk), lambda i, k: (i, k)),
            pl.BlockSpec((tk, d), lambda i, k: (k, 0)),
            pl.BlockSpec((1, d), lambda i, k: (0, 0)),
        ],
        out_specs=pl.BlockSpec((tm, d), lambda i, k: (i, 0)),
        out_shape=jax.ShapeDtypeStruct((t, d), F32),
        scratch_shapes=[pltpu.VMEM((tm, d), F32)],
        compiler_params=_params(("parallel", "arbitrary")),
        name="ffn_down",
    )(x1, act, w_down, gain)


def _layer(x2, batch, seq, norm_mix_w, w_in, fourier_w, ssm_conv_w, ssm_conv_b, dt_bias_fwd, a_log_fwd,
           dt_bias_bwd, a_log_bwd, ssm_d, ssm_norm_w, w_out, norm_ffn_w, w_up, ffn_conv_w, ffn_conv_b,
           w_down, final_gain, tiles):
    d = x2.shape[1]
    g, r = SSM_GROUPS, HEADS_PER_GROUP
    fw = fourier_w.shape[0] * fourier_w.shape[1]
    sw = g * GROUP_X
    gn = g * SSM_STATE
    main = fw + 2 * sw + 2 * gn

    wdt = jnp.pad(w_in[:, main:].reshape(d, g, r), ((0, 0), (0, 0), (0, HEAD_ROWS - r)))
    wdt = jnp.pad(wdt.reshape(d, g * HEAD_ROWS), ((0, 0), (0, 128 - g * HEAD_ROWS))).astype(BF16)

    proj, dt_t = _in_proj(x2, norm_mix_w[None, :], w_in.astype(BF16), wdt, n=main, nh=g * HEAD_ROWS,
                          tm=tiles["in_tm"], tn=tiles["in_tn"])

    a_out = _fourier_fft(proj, fourier_w, batch=batch, seq=seq, col0=main - fw)

    hp = jnp.stack([dt_bias_fwd, a_log_fwd, dt_bias_bwd, a_log_bwd], axis=-1).reshape(g, r, 4)
    hp = jnp.pad(hp, ((0, 0), (0, HEAD_ROWS - r), (0, 0)))
    d_exp = jnp.repeat(ssm_d, SSM_HEAD_DIM).reshape(g, 1, GROUP_X)
    b_out = _ssd(proj, dt_t, ssm_conv_w, ssm_conv_b[None, :], hp, d_exp, ssm_norm_w.reshape(g, 1, GROUP_X),
                 batch=batch, seq=seq)

    w_out_b = w_out.astype(BF16)
    x1 = _out_proj(x2, a_out, b_out, w_out_b[:fw], w_out_b[fw:], tm=tiles["o_tm"], tn=tiles["o_tn"])

    act = _ffn_up(x1, norm_ffn_w[None, :], w_up.astype(BF16), ffn_conv_w, ffn_conv_b[None, :],
                  seq=seq, tm=tiles["u_tm"], tn=tiles["u_tn"])
    return _ffn_down(x1, act, w_down.astype(BF16), final_gain[None, :], tm=tiles["d_tm"], tk=tiles["d_tk"])


TILES = dict(in_tm=1024, in_tn=1024, f_tm=512, o_tm=1024, o_tn=512, u_tm=1024, u_tn=512, d_tm=1024, d_tk=512)


def kernel(x, norm_mix_w, w_in, fourier_w, ssm_conv_w, ssm_conv_b, dt_bias_fwd, a_log_fwd, dt_bias_bwd,
           a_log_bwd, ssm_d, ssm_norm_w, w_out, norm_ffn_w, w_up, ffn_conv_w, ffn_conv_b, w_down,
           norm_final_w):
    batch, seq, d = x.shape
    assert norm_mix_w.shape[0] == 1, "one layer"
    out = _layer(x.reshape(batch * seq, d), batch, seq, norm_mix_w[0], w_in[0], fourier_w[0], ssm_conv_w[0],
                 ssm_conv_b[0], dt_bias_fwd[0], a_log_fwd[0], dt_bias_bwd[0], a_log_bwd[0], ssm_d[0],
                 ssm_norm_w[0], w_out[0], norm_ffn_w[0], w_up[0], ffn_conv_w[0], ffn_conv_b[0], w_down[0],
                 norm_final_w, TILES)
    return out.reshape(batch, seq, d)
```

```python
import functools

import jax
import jax.numpy as jnp
import numpy as np
from jax import lax
from jax.experimental import pallas as pl
from jax.experimental.pallas import tpu as pltpu

NORM_EPS = 1e-5
SSM_GROUPS = 8
HEADS_PER_GROUP = 6
HEAD_ROWS = 8
SSM_HEAD_DIM = 64
SSM_STATE = 128
CHUNK = 128
GROUP_X = HEADS_PER_GROUP * SSM_HEAD_DIM
GROUP_XBC = GROUP_X + 2 * SSM_STATE
HALO = 16
VMEM_LIMIT = 56 * 1024 * 1024

F32 = jnp.float32
BF16 = jnp.bfloat16


def _params(semantics):
    return pltpu.CompilerParams(dimension_semantics=semantics, vmem_limit_bytes=VMEM_LIMIT)


def _silu(v):
    return v / (1.0 + jnp.exp(-v))


def _in_proj_kernel(x_ref, g_ref, w_ref, wdt_ref, o_ref, dt_ref, h_ref, *, sub):
    @pl.when(pl.program_id(1) == 0)
    def _():
        def body(r, carry):
            r0 = pl.multiple_of(r * sub, sub)
            xs = x_ref[pl.ds(r0, sub), :]
            var = jnp.mean(xs * xs, axis=-1, keepdims=True)
            hs = xs * lax.rsqrt(var + NORM_EPS) * g_ref[...]
            h_ref[pl.ds(r0, sub), :] = hs.astype(BF16)
            return carry
        lax.fori_loop(0, x_ref.shape[0] // sub, body, 0)
        dt = jnp.dot(h_ref[...], wdt_ref[...], preferred_element_type=F32)
        lanes = wdt_ref.shape[1]
        for r in range(x_ref.shape[0] // lanes):
            dt_ref[:, r * lanes:(r + 1) * lanes] = jnp.transpose(
                dt[r * lanes:(r + 1) * lanes, :])[:dt_ref.shape[0], :]

    o_ref[...] = jnp.dot(h_ref[...], w_ref[...], preferred_element_type=F32).astype(o_ref.dtype)


def _in_proj(x2, gain, w, wdt, *, n, nh, tm, tn):
    t, d = x2.shape
    nj = n // tn
    return pl.pallas_call(
        functools.partial(_in_proj_kernel, sub=128),
        grid=(t // tm, nj),
        in_specs=[
            pl.BlockSpec((tm, d), lambda i, j: (i, 0)),
            pl.BlockSpec((1, d), lambda i, j: (0, 0)),
            pl.BlockSpec((d, tn), lambda i, j: (0, (j + 1) % nj)),
            pl.BlockSpec(wdt.shape, lambda i, j: (0, 0)),
        ],
        out_specs=[
            pl.BlockSpec((tm, tn), lambda i, j: (i, j)),
            pl.BlockSpec((nh, tm), lambda i, j: (0, i)),
        ],
        out_shape=[jax.ShapeDtypeStruct((t, n), BF16), jax.ShapeDtypeStruct((nh, t), F32)],
        scratch_shapes=[pltpu.VMEM((tm, d), BF16)],
        compiler_params=_params(("parallel", "arbitrary")),
        name="in_proj",
    )(x2, gain, w, wdt)


FFT_SPLIT = 4
FFT_COLS = 256


def _fourier_fft_kernel(u_ref, cc_ref, sc_ref, w_ref, l2_ref, twc_ref, tws_ref, o_ref, z_ref, o32_ref, *, scale):
    n2 = l2_ref.shape[0]
    c = cc_ref.shape[0]
    groups = FFT_COLS // c
    cb = pl.program_id(1)
    for gi in range(groups):
        w = w_ref[cb * groups + gi]
        mix_c = (jnp.dot(cc_ref[...], w, preferred_element_type=F32) * scale).astype(BF16)
        mix_s = (jnp.dot(sc_ref[...], w, preferred_element_type=F32) * scale).astype(BF16)
        ug = u_ref[:, gi * c:(gi + 1) * c]
        z_ref[0, :, gi * c:(gi + 1) * c] = jnp.dot(ug, mix_c, preferred_element_type=F32)
        z_ref[1, :, gi * c:(gi + 1) * c] = -jnp.dot(ug, mix_s, preferred_element_type=F32)

    zc = [z_ref[0, a * n2:(a + 1) * n2, :] for a in range(FFT_SPLIT)]
    zs = [z_ref[1, a * n2:(a + 1) * n2, :] for a in range(FFT_SPLIT)]
    ec, es, fc, fs = zc[0] + zc[2], zs[0] + zs[2], zc[0] - zc[2], zs[0] - zs[2]
    gc, gs, hc, hs = zc[1] + zc[3], zs[1] + zs[3], zc[1] - zc[3], zs[1] - zs[3]
    butterflies = [(ec + gc, es + gs), (fc + hs, fs - hc), (ec - gc, es - gs), (fc - hs, fs + hc)]
    lane_reps = FFT_COLS // twc_ref.shape[2]
    for k_lo, (yc, ys) in enumerate(butterflies):
        if k_lo > 0:
            twc = jnp.concatenate([twc_ref[k_lo - 1]] * lane_reps, axis=1)
            tws = jnp.concatenate([tws_ref[k_lo - 1]] * lane_reps, axis=1)
            yc, ys = yc * twc + ys * tws, ys * twc - yc * tws
        rhs = jnp.concatenate([yc, ys], axis=0).astype(BF16)
        x = jnp.dot(l2_ref[...], rhs, preferred_element_type=F32)
        for h in range(FFT_COLS // 128):
            o32_ref[h, pl.ds(k_lo, n2, stride=FFT_SPLIT), :] = x[:, 128 * h:128 * (h + 1)]
    for h in range(FFT_COLS // 128):
        o_ref[:, 128 * h:128 * (h + 1)] = o32_ref[h].astype(o_ref.dtype)


def _fourier_fft(proj, w_mix, *, batch, seq, col0):
    g, c, _ = w_mix.shape
    n2 = seq // FFT_SPLIT
    assert seq % FFT_SPLIT == 0 and FFT_COLS % c == 0 and (g * c) % FFT_COLS == 0 and col0 % FFT_COLS == 0
    blk0 = col0 // FFT_COLS
    scale = float(1.0 / np.sqrt(seq * c))
    cos_c, sin_c = _dft_tables(c, 8)
    cos_n, sin_n = _dft_tables(n2, 32)
    l2 = jnp.concatenate([cos_n, sin_n], axis=1).astype(BF16)
    k_lo = jnp.arange(1, FFT_SPLIT, dtype=jnp.int32)[:, None]
    ang = (k_lo * jnp.arange(n2, dtype=jnp.int32)[None, :]).astype(F32) * (2.0 * np.pi / seq)
    twc = jnp.broadcast_to(jnp.cos(ang)[:, :, None], (FFT_SPLIT - 1, n2, 128))
    tws = jnp.broadcast_to(jnp.sin(ang)[:, :, None], (FFT_SPLIT - 1, n2, 128))
    const2 = lambda b, j: (0, 0)
    const3 = lambda b, j: (0, 0, 0)
    return pl.pallas_call(
        functools.partial(_fourier_fft_kernel, scale=scale),
        grid=(batch, g * c // FFT_COLS),
        in_specs=[
            pl.BlockSpec((seq, FFT_COLS), lambda b, j: (b, blk0 + j)),
            pl.BlockSpec((c, c), const2),
            pl.BlockSpec((c, c), const2),
            pl.BlockSpec((g, c, c), const3),
            pl.BlockSpec(l2.shape, const2),
            pl.BlockSpec(twc.shape, const3),
            pl.BlockSpec(tws.shape, const3),
        ],
        out_specs=pl.BlockSpec((seq, FFT_COLS), lambda b, j: (b, j)),
        out_shape=jax.ShapeDtypeStruct((batch * seq, g * c), BF16),
        scratch_shapes=[pltpu.VMEM((2, seq, FFT_COLS), F32),
                        pltpu.VMEM((FFT_COLS // 128, seq, 128), F32)],
        compiler_params=_params(("parallel", "arbitrary")),
        name="fourier",
    )(proj, cos_c.astype(BF16), sin_c.astype(BF16), w_mix.astype(BF16), l2, twc, tws)


def _dft_tables(n, split):
    hi = n // split
    s = jnp.arange(n, dtype=jnp.int32)[None, :]
    k1 = jnp.arange(hi, dtype=jnp.int32)[:, None]
    k0 = jnp.arange(split, dtype=jnp.int32)[:, None]
    a1 = ((k1 * s * split) % n).astype(F32) * (2.0 * np.pi / n)
    a0 = ((k0 * s) % n).astype(F32) * (2.0 * np.pi / n)
    c1, s1 = jnp.cos(a1)[:, None, :], jnp.sin(a1)[:, None, :]
    c0, s0 = jnp.cos(a0)[None, :, :], jnp.sin(a0)[None, :, :]
    cos_m = (c1 * c0 - s1 * s0).reshape(n, n)
    sin_m = (s1 * c0 + c1 * s0).reshape(n, n)
    return cos_m, sin_m


PIECES_P = 3
COL_DIR = PIECES_P * HEAD_ROWS
SEL_COLS = HEADS_PER_GROUP * CHUNK
ROW_Q, ROW_YS, ROW_W, ROW_CD = 0, 1, 2, 3
ROW_TABLES = 4
LOG2E = float(np.log2(np.e))
CONV_TAPS = 5
WIN_ROWS = CHUNK + 2 * HALO


def _selection_matrices():
    sel = np.zeros((2, CHUNK, SEL_COLS), np.float32)
    for d in range(2):
        for r in range(HEADS_PER_GROUP):
            for piece in range(PIECES_P):
                sel[d, d * COL_DIR + piece * HEAD_ROWS + r, r * CHUNK:(r + 1) * CHUNK] = 1.0
    return sel


def _shift_matrix():
    taps = [k for k in range(CONV_TAPS) if k != CONV_TAPS // 2]
    m = np.zeros((len(taps) * CHUNK, WIN_ROWS), np.float32)
    for i, k in enumerate(taps):
        for t in range(CHUNK):
            m[i * CHUNK + t, HALO + t + k - CONV_TAPS // 2] = 1.0
    return m


def _softplus(v):
    return jnp.maximum(v, 0.0) + jnp.log1p(jnp.exp(-jnp.abs(v)))


def _bf16_pieces(v, n):
    out = []
    for _ in range(n - 1):
        hi = v.astype(BF16).astype(F32)
        out.append(hi)
        v = v - hi
    out.append(v)
    return out


def _ssd_kernel(z_ref, x_ref, b_ref, c_ref, dt_ref, cwx_ref, cwb_ref, cwc_ref, cbx_ref, cbb_ref, cbc_ref,
                hp_ref, d_ref, nw_ref, sel_ref, shift_ref, o_ref,
                xc_ref, xt_ref, colq_ref, rowq_ref, y_ref, st_ref, pb_ref, cbm_ref):
    seq = x_ref.shape[0]
    nchunks = seq // CHUNK
    half = nchunks // 2
    xbc_rows = lambda start, n: jnp.concatenate(
        [ref[pl.ds(start, n), :] for ref in (x_ref, b_ref, c_ref)], axis=1)
    conv_w = jnp.concatenate([cwx_ref[...], cwb_ref[...], cwc_ref[...]], axis=1)
    conv_b = jnp.concatenate([cbx_ref[...], cbb_ref[...], cbc_ref[...]], axis=1)
    li = lax.broadcasted_iota(jnp.int32, (CHUNK, CHUNK), 0)
    si = lax.broadcasted_iota(jnp.int32, (CHUNK, CHUNK), 1)
    upper_b = jnp.where(li <= si, 1.0, 0.0).astype(BF16)
    lane_lo = si < SSM_HEAD_DIM
    lane_lo_row = lax.broadcasted_iota(jnp.int32, (1, 128), 1) < SSM_HEAD_DIM

    hp = hp_ref[...]
    bias_f, a_f = hp[:, 0:1], -jnp.exp(hp[:, 1:2])
    bias_b, a_b = hp[:, 2:3], -jnp.exp(hp[:, 3:4])

    def prep(c):
        r0 = pl.multiple_of(c * CHUNK, CHUNK)
        cur = xbc_rows(r0, CHUNK)
        p0 = pl.multiple_of(jnp.maximum(r0 - HALO, 0), HALO)
        n0 = pl.multiple_of(jnp.minimum(r0 + CHUNK, seq - HALO), HALO)
        prev = xbc_rows(p0, HALO)
        nxt = xbc_rows(n0, HALO)
        prev = jnp.where(c > 0, prev, jnp.zeros_like(prev))
        nxt = jnp.where(c < nchunks - 1, nxt, jnp.zeros_like(nxt))
        win = jnp.concatenate([prev, cur, nxt], axis=0)
        taps = jnp.dot(shift_ref[...], win, preferred_element_type=F32)
        mid = CONV_TAPS // 2
        acc = cur.astype(F32) * conv_w[mid:mid + 1, :] + conv_b
        for i, k in enumerate([k for k in range(CONV_TAPS) if k != mid]):
            acc = acc + taps[i * CHUNK:(i + 1) * CHUNK, :] * conv_w[k:k + 1, :]
        act = _silu(acc)
        xc_ref[pl.ds(r0, CHUNK), :] = act.astype(BF16)
        xt_ref[:, pl.ds(r0, CHUNK)] = jnp.concatenate(
            [jnp.transpose(act[:, 128 * j:128 * (j + 1)]) for j in range(GROUP_X // 128)], axis=0).astype(BF16)

    prep_unroll = min(8, nchunks)

    def prep_group(i, carry):
        for u in range(prep_unroll):
            prep(prep_unroll * i + u)
        return carry

    lax.fori_loop(0, nchunks // prep_unroll, prep_group, 0)

    trows = nchunks * HEAD_ROWS
    dt_all = dt_ref[...]
    dtr = jnp.concatenate([dt_all[:, c * CHUNK:(c + 1) * CHUNK] for c in range(nchunks)], axis=0)
    per_row = lambda col: jnp.concatenate([jnp.broadcast_to(col, (HEAD_ROWS, CHUNK))] * nchunks, axis=0)
    lane_const = lambda v: jnp.broadcast_to(v, (trows, CHUNK))

    def cumsum(v):
        stacked = jnp.concatenate(_bf16_pieces(v, 3), axis=0).astype(BF16)
        out = jnp.dot(stacked, upper_b, preferred_element_type=F32)
        return (out[:trows] + out[trows:2 * trows]) + out[2 * trows:]

    dt_f = _softplus(dtr + per_row(bias_f))
    dt_b = _softplus(dtr + per_row(bias_b))
    adt_f = dt_f * per_row(a_f)
    adt_b = dt_b * per_row(a_b)
    cum_f = cumsum(adt_f)
    cum_b = cumsum(adt_b)
    tot_f = cum_f[:, CHUNK - 1:CHUNK]
    tot_b = cum_b[:, CHUNK - 1:CHUNK]
    ex_b = cum_b - adt_b
    p2_f = cum_f * LOG2E
    p2_b = -ex_b * LOG2E
    tables = [p2_f - jnp.log(dt_f) * LOG2E, jnp.zeros((trows, CHUNK), F32),
              jnp.exp(tot_f - cum_f) * dt_f, lane_const(jnp.exp(tot_f)),
              p2_b - jnp.log(dt_b) * LOG2E, lane_const(tot_b * LOG2E),
              jnp.exp(ex_b) * dt_b, lane_const(jnp.exp(tot_b))]
    for q, tab in enumerate(tables):
        rowq_ref[q] = tab
    pieces = _bf16_pieces(p2_f, PIECES_P) + _bf16_pieces(p2_b, PIECES_P)
    pad = jnp.zeros((CHUNK - 2 * COL_DIR, CHUNK), F32)
    for c in range(nchunks):
        rows = [p[c * HEAD_ROWS:(c + 1) * HEAD_ROWS, :] for p in pieces] + [pad]
        colq_ref[c * CHUNK:(c + 1) * CHUNK, :] = jnp.transpose(jnp.concatenate(rows, axis=0)).astype(BF16)

    nt = (((1,), (1,)), ((), ()))

    def stage_ahead(c, direction):
        r0 = pl.multiple_of(c * CHUNK, CHUNK)
        pb_ref[direction] = jnp.dot(colq_ref[pl.ds(r0, CHUNK), :], sel_ref[direction],
                                    preferred_element_type=F32)
        cbm_ref[direction] = lax.dot_general(
            xc_ref[pl.ds(r0, CHUNK), GROUP_X + SSM_STATE:], xc_ref[pl.ds(r0, CHUNK), GROUP_X:GROUP_X + SSM_STATE],
            nt, preferred_element_type=F32)

    def chunk(c, direction):
        r0 = pl.multiple_of(c * CHUNK, CHUNK)
        xcb = xc_ref[pl.ds(r0, CHUNK), :]
        xs = xcb[:, :GROUP_X]
        bm = xcb[:, GROUP_X:GROUP_X + SSM_STATE]
        cm = xcb[:, GROUP_X + SSM_STATE:]
        r8 = pl.multiple_of(c * HEAD_ROWS, HEAD_ROWS)
        tabs = [rowq_ref[direction * ROW_TABLES + q, pl.ds(r8, HEAD_ROWS), :] for q in range(ROW_TABLES)]
        row = lambda q, r: tabs[q][r:r + 1, :]
        pb = pb_ref[direction]
        cb = cbm_ref[direction]
        step = 1 if direction == 0 else -1
        stage_ahead(jnp.clip(c + step, 0, nchunks - 1), direction)
        mask = (li >= si) if direction == 0 else (si >= li)

        state_t = st_ref[direction]
        y_off = lax.dot_general(cm, state_t.astype(BF16), nt, preferred_element_type=F32)

        xs_lo = jnp.where(lane_lo_row, 1.0, 0.0).astype(BF16)
        xs_hi = jnp.where(lane_lo_row, 0.0, 1.0).astype(BF16)
        tiles = []
        for j in range(HEADS_PER_GROUP // 2):
            xp = xs[:, 128 * j:128 * (j + 1)]
            pa = pb[:, 2 * j * CHUNK:(2 * j + 1) * CHUNK]
            pc = pb[:, (2 * j + 1) * CHUNK:(2 * j + 2) * CHUNK]
            ys = jnp.exp2(jnp.where(lane_lo, pa, pc)
                          + jnp.where(lane_lo_row, row(ROW_YS, 2 * j), row(ROW_YS, 2 * j + 1)))
            ma = (cb * jnp.exp2(jnp.where(mask, pa - row(ROW_Q, 2 * j), -jnp.inf))).astype(BF16)
            mc = (cb * jnp.exp2(jnp.where(mask, pc - row(ROW_Q, 2 * j + 1), -jnp.inf))).astype(BF16)
            tiles.append(y_off[:, 128 * j:128 * (j + 1)] * ys
                         + jnp.dot(ma, xp * xs_lo, preferred_element_type=F32)
                         + jnp.dot(mc, xp * xs_hi, preferred_element_type=F32))
        y = jnp.concatenate(tiles, axis=1)

        per_head = lambda off, dtype=F32: jnp.concatenate(
            [jnp.broadcast_to(row(off, r).astype(dtype), (SSM_HEAD_DIM, CHUNK)) for r in range(HEADS_PER_GROUP)],
            axis=0)
        xw = xt_ref[:, pl.ds(r0, CHUNK)] * per_head(ROW_W, BF16)
        new = jnp.dot(xw, bm, preferred_element_type=F32)
        st_ref[direction] = state_t * per_head(ROW_CD) + new
        return r0, xs, y

    def finish(r0, xs, y):
        y = y + y_ref[pl.ds(r0, CHUNK), :] + d_ref[...] * xs.astype(F32)
        y = y * _silu(z_ref[pl.ds(r0, CHUNK), :].astype(F32))
        ms = jnp.mean(y * y, axis=-1, keepdims=True)
        o_ref[pl.ds(r0, CHUNK), :] = (y * lax.rsqrt(ms + NORM_EPS) * nw_ref[...]).astype(o_ref.dtype)

    st_ref[...] = jnp.zeros_like(st_ref)

    scan_unroll = 4 if half % 4 == 0 else (2 if half % 2 == 0 else 1)

    def first_visits(t, carry):
        for u in range(scan_unroll):
            i = scan_unroll * t + u
            for c, direction in ((i, 0), (nchunks - 1 - i, 1)):
                r0, _, y = chunk(c, direction)
                y_ref[pl.ds(r0, CHUNK), :] = y
        return carry

    def second_visits(t, carry):
        for u in range(scan_unroll):
            i = half + scan_unroll * t + u
            for c, direction in ((i, 0), (nchunks - 1 - i, 1)):
                finish(*chunk(c, direction))
        return carry

    stage_ahead(0, 0)
    stage_ahead(nchunks - 1, 1)
    lax.fori_loop(0, half // scan_unroll, first_visits, 0)
    lax.fori_loop(0, half // scan_unroll, second_visits, 0)


def _ssd(proj, dt_t, conv_w, conv_b, head_params, d_exp, norm_w, *, batch, seq):
    g = SSM_GROUPS
    width = conv_w.shape[0]
    assert width == CONV_TAPS and (seq // CHUNK) % 2 == 0
    sel = jnp.asarray(_selection_matrices(), BF16)
    shift = jnp.asarray(_shift_matrix(), BF16)
    x_blk = g
    b_blk = 2 * g * GROUP_X // SSM_STATE
    c_blk = b_blk + g
    cwb_blk = g * GROUP_X // SSM_STATE
    cwc_blk = cwb_blk + g
    return pl.pallas_call(
        _ssd_kernel,
        grid=(batch, g),
        in_specs=[
            pl.BlockSpec((seq, GROUP_X), lambda b, k: (b, k)),
            pl.BlockSpec((seq, GROUP_X), lambda b, k: (b, x_blk + k)),
            pl.BlockSpec((seq, SSM_STATE), lambda b, k: (b, b_blk + k)),
            pl.BlockSpec((seq, SSM_STATE), lambda b, k: (b, c_blk + k)),
            pl.BlockSpec((HEAD_ROWS, seq), lambda b, k: (k, b)),
            pl.BlockSpec((width, GROUP_X), lambda b, k: (0, k)),
            pl.BlockSpec((width, SSM_STATE), lambda b, k: (0, cwb_blk + k)),
            pl.BlockSpec((width, SSM_STATE), lambda b, k: (0, cwc_blk + k)),
            pl.BlockSpec((1, GROUP_X), lambda b, k: (0, k)),
            pl.BlockSpec((1, SSM_STATE), lambda b, k: (0, cwb_blk + k)),
            pl.BlockSpec((1, SSM_STATE), lambda b, k: (0, cwc_blk + k)),
            pl.BlockSpec((None, HEAD_ROWS, 4), lambda b, k: (k, 0, 0)),
            pl.BlockSpec((None, 1, GROUP_X), lambda b, k: (k, 0, 0)),
            pl.BlockSpec((None, 1, GROUP_X), lambda b, k: (k, 0, 0)),
            pl.BlockSpec(sel.shape, lambda b, k: (0, 0, 0)),
            pl.BlockSpec(shift.shape, lambda b, k: (0, 0)),
        ],
        out_specs=pl.BlockSpec((seq, GROUP_X), lambda b, k: (b, k)),
        out_shape=jax.ShapeDtypeStruct((batch * seq, g * GROUP_X), BF16),
        scratch_shapes=[
            pltpu.VMEM((seq, GROUP_XBC), BF16),
            pltpu.VMEM((GROUP_X, seq), BF16),
            pltpu.VMEM((seq, CHUNK), BF16),
            pltpu.VMEM((2 * ROW_TABLES, seq // CHUNK * HEAD_ROWS, CHUNK), F32),
            pltpu.VMEM((seq, GROUP_X), F32),
            pltpu.VMEM((2, GROUP_X, SSM_STATE), F32),
            pltpu.VMEM((2, CHUNK, SEL_COLS), F32),
            pltpu.VMEM((2, CHUNK, CHUNK), F32),
        ],
        compiler_params=_params(("parallel", "arbitrary")),
        name="ssd",
    )(proj, proj, proj, proj, dt_t, conv_w, conv_w, conv_w, conv_b, conv_b, conv_b,
      head_params, d_exp, norm_w, sel, shift)


def _out_proj_kernel(x_ref, a_ref, b_ref, w_ref, o_ref):
    ka = a_ref.shape[1]
    acc = jnp.dot(a_ref[...], w_ref[:ka, :], preferred_element_type=F32)
    acc = acc + jnp.dot(b_ref[...], w_ref[ka:, :], preferred_element_type=F32)
    o_ref[...] = x_ref[...] + acc


def _out_proj(x2, a_out, b_out, w, *, tm, tn):
    t, d = x2.shape
    ka, kb = a_out.shape[1], b_out.shape[1]
    resident = pl.Buffered(1) if tn == d else None
    return pl.pallas_call(
        _out_proj_kernel,
        grid=(t // tm, d // tn),
        in_specs=[
            pl.BlockSpec((tm, tn), lambda i, j: (i, j)),
            pl.BlockSpec((tm, ka), lambda i, j: (i, 0)),
            pl.BlockSpec((tm, kb), lambda i, j: (i, 0)),
            pl.BlockSpec((ka + kb, tn), lambda i, j: (0, j), pipeline_mode=resident),
        ],
        out_specs=pl.BlockSpec((tm, tn), lambda i, j: (i, j)),
        out_shape=jax.ShapeDtypeStruct((t, d), F32),
        compiler_params=_params(("parallel", "arbitrary")),
        name="out_proj",
    )(x2, a_out, b_out, w)


FFN_HALO = HALO


def _ffn_up_kernel(x_ref, xp_ref, xn_ref, g_ref, wg_ref, wv_ref, cwg_ref, cwv_ref, cbg_ref, cbv_ref,
                   o_ref, h_ref, *, tiles_per_seq):
    tm = x_ref.shape[0]
    i = pl.program_id(0)

    def norm(v):
        var = jnp.mean(v * v, axis=-1, keepdims=True)
        return v * lax.rsqrt(var + NORM_EPS) * g_ref[...]

    @pl.when(pl.program_id(1) == 0)
    def _():
        first = (i % tiles_per_seq) == 0
        last = (i % tiles_per_seq) == tiles_per_seq - 1
        h_ref[0:FFN_HALO, :] = jnp.where(first, 0.0, norm(xp_ref[...])).astype(BF16)
        h_ref[FFN_HALO + tm:, :] = jnp.where(last, 0.0, norm(xn_ref[...])).astype(BF16)

        def body(r, carry):
            r0 = pl.multiple_of(r * 128, 128)
            h_ref[pl.ds(FFN_HALO + r0, 128), :] = norm(x_ref[pl.ds(r0, 128), :]).astype(BF16)
            return carry
        lax.fori_loop(0, tm // 128, body, 0)

    rows = tm + 2 * FFN_HALO

    def conv(w_ref, cw_ref, cb_ref):
        up = jnp.dot(h_ref[...], w_ref[...], preferred_element_type=F32)
        acc = up[FFN_HALO:FFN_HALO + tm, :] * cw_ref[1:2, :] + cb_ref[...]
        acc = acc + pltpu.roll(up, 1, axis=0)[FFN_HALO:FFN_HALO + tm, :] * cw_ref[0:1, :]
        acc = acc + pltpu.roll(up, rows - 1, axis=0)[FFN_HALO:FFN_HALO + tm, :] * cw_ref[2:3, :]
        return acc

    gate = conv(wg_ref, cwg_ref, cbg_ref)
    val = conv(wv_ref, cwv_ref, cbv_ref)
    o_ref[...] = (_silu(gate) * val).astype(o_ref.dtype)


def _ffn_up(x1, gain, w_up, conv_w, conv_b, *, seq, tm, tn):
    t, d = x1.shape
    f = w_up.shape[1] // 2
    nj = f // tn
    tiles_per_seq = seq // tm
    hb = tm // FFN_HALO
    nhb = t // FFN_HALO
    return pl.pallas_call(
        functools.partial(_ffn_up_kernel, tiles_per_seq=tiles_per_seq),
        grid=(t // tm, nj),
        in_specs=[
            pl.BlockSpec((tm, d), lambda i, j: (i, 0)),
            pl.BlockSpec((FFN_HALO, d), lambda i, j: (jnp.maximum(i * hb - 1, 0), 0)),
            pl.BlockSpec((FFN_HALO, d), lambda i, j: (jnp.minimum((i + 1) * hb, nhb - 1), 0)),
            pl.BlockSpec((1, d), lambda i, j: (0, 0)),
            pl.BlockSpec((d, tn), lambda i, j: (0, j)),
            pl.BlockSpec((d, tn), lambda i, j: (0, j + nj)),
            pl.BlockSpec((3, tn), lambda i, j: (0, j)),
            pl.BlockSpec((3, tn), lambda i, j: (0, j + nj)),
            pl.BlockSpec((1, tn), lambda i, j: (0, j)),
            pl.BlockSpec((1, tn), lambda i, j: (0, j + nj)),
        ],
        out_specs=pl.BlockSpec((tm, tn), lambda i, j: (i, j)),
        out_shape=jax.ShapeDtypeStruct((t, f), BF16),
        scratch_shapes=[pltpu.VMEM((tm + 2 * FFN_HALO, d), BF16)],
        compiler_params=_params(("parallel", "arbitrary")),
        name="ffn_up",
    )(x1, x1, x1, gain, w_up, w_up, conv_w, conv_w, conv_b, conv_b)


def _ffn_down_kernel(x_ref, a_ref, w_ref, g_ref, o_ref):
    v = x_ref[...] + jnp.dot(a_ref[...], w_ref[...], preferred_element_type=F32)
    var = jnp.mean(v * v, axis=-1, keepdims=True)
    o_ref[...] = v * lax.rsqrt(var + NORM_EPS) * g_ref[...]


def _ffn_down(x1, act, w_down, gain, *, tm):
    t, d = x1.shape
    f = act.shape[1]
    return pl.pallas_call(
        _ffn_down_kernel,
        grid=(t // tm,),
        in_specs=[
            pl.BlockSpec((tm, d), lambda i: (i, 0)),
            pl.BlockSpec((tm, f), lambda i: (i, 0)),
            pl.BlockSpec((f, d), lambda i: (0, 0), pipeline_mode=pl.Buffered(1)),
            pl.BlockSpec((1, d), lambda i: (0, 0)),
        ],
        out_specs=pl.BlockSpec((tm, d), lambda i: (i, 0)),
        out_shape=jax.ShapeDtypeStruct((t, d), F32),
        compiler_params=_params(("parallel",)),
        name="ffn_down",
    )(x1, act, w_down, gain)


def _layer(x2, batch, seq, norm_mix_w, w_in, fourier_w, ssm_conv_w, ssm_conv_b, dt_bias_fwd, a_log_fwd,
           dt_bias_bwd, a_log_bwd, ssm_d, ssm_norm_w, w_out, norm_ffn_w, w_up, ffn_conv_w, ffn_conv_b,
           w_down, final_gain, tiles):
    d = x2.shape[1]
    g, r = SSM_GROUPS, HEADS_PER_GROUP
    fw = fourier_w.shape[0] * fourier_w.shape[1]
    sw = g * GROUP_X
    gn = g * SSM_STATE
    main = fw + 2 * sw + 2 * gn
    assert fw == tiles["in_tn"], "u must be exactly the first weight tile for the rotated read"

    wdt = jnp.pad(w_in[:, main:].reshape(d, g, r), ((0, 0), (0, 0), (0, HEAD_ROWS - r)))
    wdt = jnp.pad(wdt.reshape(d, g * HEAD_ROWS), ((0, 0), (0, 128 - g * HEAD_ROWS))).astype(BF16)

    proj, dt_t = _in_proj(x2, norm_mix_w[None, :], w_in.astype(BF16), wdt, n=main, nh=g * HEAD_ROWS,
                          tm=tiles["in_tm"], tn=tiles["in_tn"])

    a_out = _fourier_fft(proj, fourier_w, batch=batch, seq=seq, col0=main - fw)

    hp = jnp.stack([dt_bias_fwd, a_log_fwd, dt_bias_bwd, a_log_bwd], axis=-1).reshape(g, r, 4)
    hp = jnp.pad(hp, ((0, 0), (0, HEAD_ROWS - r), (0, 0)))
    d_exp = jnp.repeat(ssm_d, SSM_HEAD_DIM).reshape(g, 1, GROUP_X)
    b_out = _ssd(proj, dt_t, ssm_conv_w, ssm_conv_b[None, :], hp, d_exp, ssm_norm_w.reshape(g, 1, GROUP_X),
                 batch=batch, seq=seq)

    x1 = _out_proj(x2, a_out, b_out, w_out.astype(BF16), tm=tiles["o_tm"], tn=min(tiles["o_tn"], d))

    act = _ffn_up(x1, norm_ffn_w[None, :], w_up.astype(BF16), ffn_conv_w, ffn_conv_b[None, :],
                  seq=seq, tm=tiles["u_tm"], tn=tiles["u_tn"])
    return _ffn_down(x1, act, w_down.astype(BF16), final_gain[None, :], tm=tiles["d_tm"])


TILES = dict(in_tm=1024, in_tn=1024, o_tm=512, o_tn=2048, u_tm=1024, u_tn=512, d_tm=256)


def kernel(x, norm_mix_w, w_in, fourier_w, ssm_conv_w, ssm_conv_b, dt_bias_fwd, a_log_fwd, dt_bias_bwd,
           a_log_bwd, ssm_d, ssm_norm_w, w_out, norm_ffn_w, w_up, ffn_conv_w, ffn_conv_b, w_down,
           norm_final_w):
    batch, seq, d = x.shape
    assert norm_mix_w.shape[0] == 1, "one layer"
    out = _layer(x.reshape(batch * seq, d), batch, seq, norm_mix_w[0], w_in[0], fourier_w[0], ssm_conv_w[0],
                 ssm_conv_b[0], dt_bias_fwd[0], a_log_fwd[0], dt_bias_bwd[0], a_log_bwd[0], ssm_d[0],
                 ssm_norm_w[0], w_out[0], norm_ffn_w[0], w_up[0], ffn_conv_w[0], ffn_conv_b[0], w_down[0],
                 norm_final_w, TILES)
    return out.reshape(batch, seq, d)
```

```python
import functools

import jax
import jax.numpy as jnp
import numpy as np
from jax import lax
from jax.experimental import pallas as pl
from jax.experimental.pallas import tpu as pltpu

NORM_EPS = 1e-5
SSM_GROUPS = 8
HEADS_PER_GROUP = 6
HEAD_ROWS = 8
SSM_HEAD_DIM = 64
SSM_STATE = 128
CHUNK = 128
GROUP_X = HEADS_PER_GROUP * SSM_HEAD_DIM
GROUP_XBC = GROUP_X + 2 * SSM_STATE
HALO = 16
VMEM_LIMIT = 56 * 1024 * 1024

F32 = jnp.float32
BF16 = jnp.bfloat16


def _params(semantics):
    return pltpu.CompilerParams(dimension_semantics=semantics, vmem_limit_bytes=VMEM_LIMIT)


def _silu(v):
    return v / (1.0 + jnp.exp(-v))


def _in_proj_kernel(x_ref, g_ref, w_ref, wdt_ref, o_ref, dt_ref, h_ref, *, sub):
    @pl.when(pl.program_id(1) == 0)
    def _():
        def body(r, carry):
            r0 = pl.multiple_of(r * sub, sub)
            xs = x_ref[pl.ds(r0, sub), :]
            var = jnp.mean(xs * xs, axis=-1, keepdims=True)
            hs = xs * lax.rsqrt(var + NORM_EPS) * g_ref[...]
            h_ref[pl.ds(r0, sub), :] = hs.astype(BF16)
            return carry
        lax.fori_loop(0, x_ref.shape[0] // sub, body, 0)
        dt = jnp.dot(h_ref[...], wdt_ref[...], preferred_element_type=F32)
        lanes = wdt_ref.shape[1]
        for r in range(x_ref.shape[0] // lanes):
            dt_ref[:, r * lanes:(r + 1) * lanes] = jnp.transpose(
                dt[r * lanes:(r + 1) * lanes, :])[:dt_ref.shape[0], :]

    o_ref[...] = jnp.dot(h_ref[...], w_ref[...].astype(BF16), preferred_element_type=F32).astype(o_ref.dtype)


def _in_proj(x2, gain, w, wdt, *, n, nh, tm, tn):
    t, d = x2.shape
    nj = n // tn
    return pl.pallas_call(
        functools.partial(_in_proj_kernel, sub=128),
        grid=(t // tm, nj),
        in_specs=[
            pl.BlockSpec((tm, d), lambda i, j: (i, 0)),
            pl.BlockSpec((1, d), lambda i, j: (0, 0)),
            pl.BlockSpec((d, tn), lambda i, j: (0, (j + 1) % nj)),
            pl.BlockSpec(wdt.shape, lambda i, j: (0, 0)),
        ],
        out_specs=[
            pl.BlockSpec((tm, tn), lambda i, j: (i, j)),
            pl.BlockSpec((nh, tm), lambda i, j: (0, i)),
        ],
        out_shape=[jax.ShapeDtypeStruct((t, n), BF16), jax.ShapeDtypeStruct((nh, t), F32)],
        scratch_shapes=[pltpu.VMEM((tm, d), BF16)],
        compiler_params=_params(("parallel", "arbitrary")),
        name="in_proj",
    )(x2, gain, w, wdt)


FFT_SPLIT = 4
FFT_COLS = 256


def _fourier_fft_kernel(u_ref, cc_ref, sc_ref, w_ref, l2_ref, twc_ref, tws_ref, o_ref, z_ref, o32_ref, *, scale):
    n2 = l2_ref.shape[0]
    c = cc_ref.shape[0]
    groups = FFT_COLS // c
    cb = pl.program_id(1)
    for gi in range(groups):
        w = w_ref[cb * groups + gi]
        mix_c = (jnp.dot(cc_ref[...], w, preferred_element_type=F32) * scale).astype(BF16)
        mix_s = (jnp.dot(sc_ref[...], w, preferred_element_type=F32) * scale).astype(BF16)
        ug = u_ref[:, gi * c:(gi + 1) * c]
        z_ref[0, :, gi * c:(gi + 1) * c] = jnp.dot(ug, mix_c, preferred_element_type=F32)
        z_ref[1, :, gi * c:(gi + 1) * c] = -jnp.dot(ug, mix_s, preferred_element_type=F32)

    zc = [z_ref[0, a * n2:(a + 1) * n2, :] for a in range(FFT_SPLIT)]
    zs = [z_ref[1, a * n2:(a + 1) * n2, :] for a in range(FFT_SPLIT)]
    ec, es, fc, fs = zc[0] + zc[2], zs[0] + zs[2], zc[0] - zc[2], zs[0] - zs[2]
    gc, gs, hc, hs = zc[1] + zc[3], zs[1] + zs[3], zc[1] - zc[3], zs[1] - zs[3]
    butterflies = [(ec + gc, es + gs), (fc + hs, fs - hc), (ec - gc, es - gs), (fc - hs, fs + hc)]
    lane_reps = FFT_COLS // twc_ref.shape[2]
    for k_lo, (yc, ys) in enumerate(butterflies):
        if k_lo > 0:
            twc = jnp.concatenate([twc_ref[k_lo - 1]] * lane_reps, axis=1)
            tws = jnp.concatenate([tws_ref[k_lo - 1]] * lane_reps, axis=1)
            yc, ys = yc * twc + ys * tws, ys * twc - yc * tws
        rhs = jnp.concatenate([yc, ys], axis=0).astype(BF16)
        x = jnp.dot(l2_ref[...], rhs, preferred_element_type=F32)
        for h in range(FFT_COLS // 128):
            o32_ref[h, pl.ds(k_lo, n2, stride=FFT_SPLIT), :] = x[:, 128 * h:128 * (h + 1)]
    for h in range(FFT_COLS // 128):
        o_ref[:, 128 * h:128 * (h + 1)] = o32_ref[h].astype(o_ref.dtype)


def _fourier_fft(proj, w_mix, *, batch, seq, col0):
    g, c, _ = w_mix.shape
    n2 = seq // FFT_SPLIT
    assert seq % FFT_SPLIT == 0 and FFT_COLS % c == 0 and (g * c) % FFT_COLS == 0 and col0 % FFT_COLS == 0
    blk0 = col0 // FFT_COLS
    scale = float(1.0 / np.sqrt(seq * c))
    cos_c, sin_c = _dft_tables(c, 8)
    cos_n, sin_n = _dft_tables(n2, 32)
    l2 = jnp.concatenate([cos_n, sin_n], axis=1).astype(BF16)
    k_lo = jnp.arange(1, FFT_SPLIT, dtype=jnp.int32)[:, None]
    ang = (k_lo * jnp.arange(n2, dtype=jnp.int32)[None, :]).astype(F32) * (2.0 * np.pi / seq)
    twc = jnp.broadcast_to(jnp.cos(ang)[:, :, None], (FFT_SPLIT - 1, n2, 128))
    tws = jnp.broadcast_to(jnp.sin(ang)[:, :, None], (FFT_SPLIT - 1, n2, 128))
    const2 = lambda b, j: (0, 0)
    const3 = lambda b, j: (0, 0, 0)
    return pl.pallas_call(
        functools.partial(_fourier_fft_kernel, scale=scale),
        grid=(batch, g * c // FFT_COLS),
        in_specs=[
            pl.BlockSpec((seq, FFT_COLS), lambda b, j: (b, blk0 + j)),
            pl.BlockSpec((c, c), const2),
            pl.BlockSpec((c, c), const2),
            pl.BlockSpec((g, c, c), const3),
            pl.BlockSpec(l2.shape, const2),
            pl.BlockSpec(twc.shape, const3),
            pl.BlockSpec(tws.shape, const3),
        ],
        out_specs=pl.BlockSpec((seq, FFT_COLS), lambda b, j: (b, j)),
        out_shape=jax.ShapeDtypeStruct((batch * seq, g * c), BF16),
        scratch_shapes=[pltpu.VMEM((2, seq, FFT_COLS), F32),
                        pltpu.VMEM((FFT_COLS // 128, seq, 128), F32)],
        compiler_params=_params(("parallel", "arbitrary")),
        name="fourier",
    )(proj, cos_c.astype(BF16), sin_c.astype(BF16), w_mix.astype(BF16), l2, twc, tws)


def _dft_tables(n, split):
    hi = n // split
    s = jnp.arange(n, dtype=jnp.int32)[None, :]
    k1 = jnp.arange(hi, dtype=jnp.int32)[:, None]
    k0 = jnp.arange(split, dtype=jnp.int32)[:, None]
    a1 = ((k1 * s * split) % n).astype(F32) * (2.0 * np.pi / n)
    a0 = ((k0 * s) % n).astype(F32) * (2.0 * np.pi / n)
    c1, s1 = jnp.cos(a1)[:, None, :], jnp.sin(a1)[:, None, :]
    c0, s0 = jnp.cos(a0)[None, :, :], jnp.sin(a0)[None, :, :]
    cos_m = (c1 * c0 - s1 * s0).reshape(n, n)
    sin_m = (s1 * c0 + c1 * s0).reshape(n, n)
    return cos_m, sin_m


PIECES_P = 3
COL_DIR = PIECES_P * HEAD_ROWS
SEL_COLS = HEADS_PER_GROUP * CHUNK
ROW_Q, ROW_YS, ROW_W, ROW_CD = 0, 1, 2, 3
ROW_TABLES = 4
LOG2E = float(np.log2(np.e))
CONV_TAPS = 5
WIN_ROWS = CHUNK + 2 * HALO


def _selection_matrices():
    sel = np.zeros((2, CHUNK, SEL_COLS), np.float32)
    for d in range(2):
        for r in range(HEADS_PER_GROUP):
            for piece in range(PIECES_P):
                sel[d, d * COL_DIR + piece * HEAD_ROWS + r, r * CHUNK:(r + 1) * CHUNK] = 1.0
    return sel


def _shift_matrix():
    taps = [k for k in range(CONV_TAPS) if k != CONV_TAPS // 2]
    m = np.zeros((len(taps) * CHUNK, WIN_ROWS), np.float32)
    for i, k in enumerate(taps):
        for t in range(CHUNK):
            m[i * CHUNK + t, HALO + t + k - CONV_TAPS // 2] = 1.0
    return m


def _softplus(v):
    return jnp.maximum(v, 0.0) + jnp.log1p(jnp.exp(-jnp.abs(v)))


def _bf16_pieces(v, n):
    out = []
    for _ in range(n - 1):
        hi = v.astype(BF16).astype(F32)
        out.append(hi)
        v = v - hi
    out.append(v)
    return out


def _ssd_kernel(z_ref, x_ref, b_ref, c_ref, dt_ref, cwx_ref, cwb_ref, cwc_ref, cbx_ref, cbb_ref, cbc_ref,
                hp_ref, d_ref, nw_ref, sel_ref, shift_ref, o_ref,
                xc_ref, xt_ref, colq_ref, rowq_ref, y_ref, st_ref, pb_ref, cbm_ref):
    seq = x_ref.shape[0]
    nchunks = seq // CHUNK
    half = nchunks // 2
    xbc_rows = lambda start, n: jnp.concatenate(
        [ref[pl.ds(start, n), :] for ref in (x_ref, b_ref, c_ref)], axis=1)
    conv_w = jnp.concatenate([cwx_ref[...], cwb_ref[...], cwc_ref[...]], axis=1)
    conv_b = jnp.concatenate([cbx_ref[...], cbb_ref[...], cbc_ref[...]], axis=1)
    li = lax.broadcasted_iota(jnp.int32, (CHUNK, CHUNK), 0)
    si = lax.broadcasted_iota(jnp.int32, (CHUNK, CHUNK), 1)
    upper_b = jnp.where(li <= si, 1.0, 0.0).astype(BF16)
    lane_lo = si < SSM_HEAD_DIM
    lane_lo_row = lax.broadcasted_iota(jnp.int32, (1, 128), 1) < SSM_HEAD_DIM

    hp = hp_ref[...]
    bias_f, a_f = hp[:, 0:1], -jnp.exp(hp[:, 1:2])
    bias_b, a_b = hp[:, 2:3], -jnp.exp(hp[:, 3:4])

    def prep(c):
        r0 = pl.multiple_of(c * CHUNK, CHUNK)
        cur = xbc_rows(r0, CHUNK)
        p0 = pl.multiple_of(jnp.maximum(r0 - HALO, 0), HALO)
        n0 = pl.multiple_of(jnp.minimum(r0 + CHUNK, seq - HALO), HALO)
        prev = xbc_rows(p0, HALO)
        nxt = xbc_rows(n0, HALO)
        prev = jnp.where(c > 0, prev, jnp.zeros_like(prev))
        nxt = jnp.where(c < nchunks - 1, nxt, jnp.zeros_like(nxt))
        win = jnp.concatenate([prev, cur, nxt], axis=0)
        taps = jnp.dot(shift_ref[...], win, preferred_element_type=F32)
        mid = CONV_TAPS // 2
        acc = cur.astype(F32) * conv_w[mid:mid + 1, :] + conv_b
        for i, k in enumerate([k for k in range(CONV_TAPS) if k != mid]):
            acc = acc + taps[i * CHUNK:(i + 1) * CHUNK, :] * conv_w[k:k + 1, :]
        act = _silu(acc)
        xc_ref[pl.ds(r0, CHUNK), :] = act.astype(BF16)
        xt_ref[:, pl.ds(r0, CHUNK)] = jnp.concatenate(
            [jnp.transpose(act[:, 128 * j:128 * (j + 1)]) for j in range(GROUP_X // 128)], axis=0).astype(BF16)

    prep_unroll = min(8, nchunks)

    def prep_group(i, carry):
        for u in range(prep_unroll):
            prep(prep_unroll * i + u)
        return carry

    lax.fori_loop(0, nchunks // prep_unroll, prep_group, 0)

    trows = nchunks * HEAD_ROWS
    dt_all = dt_ref[...]
    dtr = jnp.concatenate([dt_all[:, c * CHUNK:(c + 1) * CHUNK] for c in range(nchunks)], axis=0)
    per_row = lambda col: jnp.concatenate([jnp.broadcast_to(col, (HEAD_ROWS, CHUNK))] * nchunks, axis=0)
    lane_const = lambda v: jnp.broadcast_to(v, (trows, CHUNK))

    def cumsum(v):
        stacked = jnp.concatenate(_bf16_pieces(v, 3), axis=0).astype(BF16)
        out = jnp.dot(stacked, upper_b, preferred_element_type=F32)
        return (out[:trows] + out[trows:2 * trows]) + out[2 * trows:]

    dt_f = _softplus(dtr + per_row(bias_f))
    dt_b = _softplus(dtr + per_row(bias_b))
    adt_f = dt_f * per_row(a_f)
    adt_b = dt_b * per_row(a_b)
    cum_f = cumsum(adt_f)
    cum_b = cumsum(adt_b)
    tot_f = cum_f[:, CHUNK - 1:CHUNK]
    tot_b = cum_b[:, CHUNK - 1:CHUNK]
    ex_b = cum_b - adt_b
    p2_f = cum_f * LOG2E
    p2_b = -ex_b * LOG2E
    tables = [p2_f - jnp.log(dt_f) * LOG2E, jnp.zeros((trows, CHUNK), F32),
              jnp.exp(tot_f - cum_f) * dt_f, lane_const(jnp.exp(tot_f)),
              p2_b - jnp.log(dt_b) * LOG2E, lane_const(tot_b * LOG2E),
              jnp.exp(ex_b) * dt_b, lane_const(jnp.exp(tot_b))]
    for q, tab in enumerate(tables):
        rowq_ref[q] = tab
    pieces = _bf16_pieces(p2_f, PIECES_P) + _bf16_pieces(p2_b, PIECES_P)
    pad = jnp.zeros((CHUNK - 2 * COL_DIR, CHUNK), F32)
    for c in range(nchunks):
        rows = [p[c * HEAD_ROWS:(c + 1) * HEAD_ROWS, :] for p in pieces] + [pad]
        colq_ref[c * CHUNK:(c + 1) * CHUNK, :] = jnp.transpose(jnp.concatenate(rows, axis=0)).astype(BF16)

    nt = (((1,), (1,)), ((), ()))

    def stage_ahead(c, direction):
        r0 = pl.multiple_of(c * CHUNK, CHUNK)
        pb_ref[direction] = jnp.dot(colq_ref[pl.ds(r0, CHUNK), :], sel_ref[direction],
                                    preferred_element_type=F32)
        cbm_ref[direction] = lax.dot_general(
            xc_ref[pl.ds(r0, CHUNK), GROUP_X + SSM_STATE:], xc_ref[pl.ds(r0, CHUNK), GROUP_X:GROUP_X + SSM_STATE],
            nt, preferred_element_type=F32)

    def chunk(c, direction):
        r0 = pl.multiple_of(c * CHUNK, CHUNK)
        xcb = xc_ref[pl.ds(r0, CHUNK), :]
        xs = xcb[:, :GROUP_X]
        bm = xcb[:, GROUP_X:GROUP_X + SSM_STATE]
        cm = xcb[:, GROUP_X + SSM_STATE:]
        r8 = pl.multiple_of(c * HEAD_ROWS, HEAD_ROWS)
        tabs = [rowq_ref[direction * ROW_TABLES + q, pl.ds(r8, HEAD_ROWS), :] for q in range(ROW_TABLES)]
        row = lambda q, r: tabs[q][r:r + 1, :]
        pb = pb_ref[direction]
        cb = cbm_ref[direction]
        step = 1 if direction == 0 else -1
        stage_ahead(jnp.clip(c + step, 0, nchunks - 1), direction)
        mask = (li >= si) if direction == 0 else (si >= li)

        state_t = st_ref[direction]
        y_off = lax.dot_general(cm, state_t.astype(BF16), nt, preferred_element_type=F32)

        xs_lo = jnp.where(lane_lo_row, 1.0, 0.0).astype(BF16)
        xs_hi = jnp.where(lane_lo_row, 0.0, 1.0).astype(BF16)
        tiles = []
        for j in range(HEADS_PER_GROUP // 2):
            xp = xs[:, 128 * j:128 * (j + 1)]
            pa = pb[:, 2 * j * CHUNK:(2 * j + 1) * CHUNK]
            pc = pb[:, (2 * j + 1) * CHUNK:(2 * j + 2) * CHUNK]
            ys = jnp.exp2(jnp.where(lane_lo, pa, pc)
                          + jnp.where(lane_lo_row, row(ROW_YS, 2 * j), row(ROW_YS, 2 * j + 1)))
            ma = (cb * jnp.exp2(jnp.where(mask, pa - row(ROW_Q, 2 * j), -jnp.inf))).astype(BF16)
            mc = (cb * jnp.exp2(jnp.where(mask, pc - row(ROW_Q, 2 * j + 1), -jnp.inf))).astype(BF16)
            tiles.append(y_off[:, 128 * j:128 * (j + 1)] * ys
                         + jnp.dot(ma, xp * xs_lo, preferred_element_type=F32)
                         + jnp.dot(mc, xp * xs_hi, preferred_element_type=F32))
        y = jnp.concatenate(tiles, axis=1)

        per_head = lambda off, dtype=F32: jnp.concatenate(
            [jnp.broadcast_to(row(off, r).astype(dtype), (SSM_HEAD_DIM, CHUNK)) for r in range(HEADS_PER_GROUP)],
            axis=0)
        xw = xt_ref[:, pl.ds(r0, CHUNK)] * per_head(ROW_W, BF16)
        new = jnp.dot(xw, bm, preferred_element_type=F32)
        st_ref[direction] = state_t * per_head(ROW_CD) + new
        return r0, xs, y

    def finish(r0, xs, y):
        y = y + y_ref[pl.ds(r0, CHUNK), :] + d_ref[...] * xs.astype(F32)
        y = y * _silu(z_ref[pl.ds(r0, CHUNK), :].astype(F32))
        ms = jnp.mean(y * y, axis=-1, keepdims=True)
        o_ref[pl.ds(r0, CHUNK), :] = (y * lax.rsqrt(ms + NORM_EPS) * nw_ref[...]).astype(o_ref.dtype)

    st_ref[...] = jnp.zeros_like(st_ref)

    scan_unroll = 4 if half % 4 == 0 else (2 if half % 2 == 0 else 1)

    def first_visits(t, carry):
        for u in range(scan_unroll):
            i = scan_unroll * t + u
            for c, direction in ((i, 0), (nchunks - 1 - i, 1)):
                r0, _, y = chunk(c, direction)
                y_ref[pl.ds(r0, CHUNK), :] = y
        return carry

    def second_visits(t, carry):
        for u in range(scan_unroll):
            i = half + scan_unroll * t + u
            for c, direction in ((i, 0), (nchunks - 1 - i, 1)):
                finish(*chunk(c, direction))
        return carry

    stage_ahead(0, 0)
    stage_ahead(nchunks - 1, 1)
    lax.fori_loop(0, half // scan_unroll, first_visits, 0)
    lax.fori_loop(0, half // scan_unroll, second_visits, 0)


def _ssd(proj, dt_t, conv_w, conv_b, head_params, d_exp, norm_w, *, batch, seq):
    g = SSM_GROUPS
    width = conv_w.shape[0]
    assert width == CONV_TAPS and (seq // CHUNK) % 2 == 0
    sel = jnp.asarray(_selection_matrices(), BF16)
    shift = jnp.asarray(_shift_matrix(), BF16)
    x_blk = g
    b_blk = 2 * g * GROUP_X // SSM_STATE
    c_blk = b_blk + g
    cwb_blk = g * GROUP_X // SSM_STATE
    cwc_blk = cwb_blk + g
    return pl.pallas_call(
        _ssd_kernel,
        grid=(batch, g),
        in_specs=[
            pl.BlockSpec((seq, GROUP_X), lambda b, k: (b, k)),
            pl.BlockSpec((seq, GROUP_X), lambda b, k: (b, x_blk + k)),
            pl.BlockSpec((seq, SSM_STATE), lambda b, k: (b, b_blk + k)),
            pl.BlockSpec((seq, SSM_STATE), lambda b, k: (b, c_blk + k)),
            pl.BlockSpec((HEAD_ROWS, seq), lambda b, k: (k, b)),
            pl.BlockSpec((width, GROUP_X), lambda b, k: (0, k)),
            pl.BlockSpec((width, SSM_STATE), lambda b, k: (0, cwb_blk + k)),
            pl.BlockSpec((width, SSM_STATE), lambda b, k: (0, cwc_blk + k)),
            pl.BlockSpec((1, GROUP_X), lambda b, k: (0, k)),
            pl.BlockSpec((1, SSM_STATE), lambda b, k: (0, cwb_blk + k)),
            pl.BlockSpec((1, SSM_STATE), lambda b, k: (0, cwc_blk + k)),
            pl.BlockSpec((None, HEAD_ROWS, 4), lambda b, k: (k, 0, 0)),
            pl.BlockSpec((None, 1, GROUP_X), lambda b, k: (k, 0, 0)),
            pl.BlockSpec((None, 1, GROUP_X), lambda b, k: (k, 0, 0)),
            pl.BlockSpec(sel.shape, lambda b, k: (0, 0, 0)),
            pl.BlockSpec(shift.shape, lambda b, k: (0, 0)),
        ],
        out_specs=pl.BlockSpec((seq, GROUP_X), lambda b, k: (b, k)),
        out_shape=jax.ShapeDtypeStruct((batch * seq, g * GROUP_X), BF16),
        scratch_shapes=[
            pltpu.VMEM((seq, GROUP_XBC), BF16),
            pltpu.VMEM((GROUP_X, seq), BF16),
            pltpu.VMEM((seq, CHUNK), BF16),
            pltpu.VMEM((2 * ROW_TABLES, seq // CHUNK * HEAD_ROWS, CHUNK), F32),
            pltpu.VMEM((seq, GROUP_X), F32),
            pltpu.VMEM((2, GROUP_X, SSM_STATE), F32),
            pltpu.VMEM((2, CHUNK, SEL_COLS), F32),
            pltpu.VMEM((2, CHUNK, CHUNK), F32),
        ],
        compiler_params=_params(("parallel", "arbitrary")),
        name="ssd",
    )(proj, proj, proj, proj, dt_t, conv_w, conv_w, conv_w, conv_b, conv_b, conv_b,
      head_params, d_exp, norm_w, sel, shift)


def _out_proj_kernel(x_ref, a_ref, b_ref, w_ref, o_ref):
    ka = a_ref.shape[1]
    acc = jnp.dot(a_ref[...], w_ref[:ka, :], preferred_element_type=F32)
    acc = acc + jnp.dot(b_ref[...], w_ref[ka:, :], preferred_element_type=F32)
    o_ref[...] = x_ref[...] + acc


def _out_proj(x2, a_out, b_out, w, *, tm, tn):
    t, d = x2.shape
    ka, kb = a_out.shape[1], b_out.shape[1]
    resident = pl.Buffered(1) if tn == d else None
    return pl.pallas_call(
        _out_proj_kernel,
        grid=(t // tm, d // tn),
        in_specs=[
            pl.BlockSpec((tm, tn), lambda i, j: (i, j)),
            pl.BlockSpec((tm, ka), lambda i, j: (i, 0)),
            pl.BlockSpec((tm, kb), lambda i, j: (i, 0)),
            pl.BlockSpec((ka + kb, tn), lambda i, j: (0, j), pipeline_mode=resident),
        ],
        out_specs=pl.BlockSpec((tm, tn), lambda i, j: (i, j)),
        out_shape=jax.ShapeDtypeStruct((t, d), F32),
        compiler_params=_params(("parallel", "arbitrary")),
        name="out_proj",
    )(x2, a_out, b_out, w)


FFN_HALO = HALO


def _ffn_up_kernel(x_ref, xp_ref, xn_ref, g_ref, wg_ref, wv_ref, cwg_ref, cwv_ref, cbg_ref, cbv_ref,
                   o_ref, h_ref, *, tiles_per_seq):
    tm = x_ref.shape[0]
    i = pl.program_id(0)

    def norm(v):
        var = jnp.mean(v * v, axis=-1, keepdims=True)
        return v * lax.rsqrt(var + NORM_EPS) * g_ref[...]

    @pl.when(pl.program_id(1) == 0)
    def _():
        first = (i % tiles_per_seq) == 0
        last = (i % tiles_per_seq) == tiles_per_seq - 1
        h_ref[0:FFN_HALO, :] = jnp.where(first, 0.0, norm(xp_ref[...])).astype(BF16)
        h_ref[FFN_HALO + tm:, :] = jnp.where(last, 0.0, norm(xn_ref[...])).astype(BF16)

        def body(r, carry):
            r0 = pl.multiple_of(r * 128, 128)
            h_ref[pl.ds(FFN_HALO + r0, 128), :] = norm(x_ref[pl.ds(r0, 128), :]).astype(BF16)
            return carry
        lax.fori_loop(0, tm // 128, body, 0)

    rows = tm + 2 * FFN_HALO

    def conv(w_ref, cw_ref, cb_ref):
        up = jnp.dot(h_ref[...], w_ref[...].astype(BF16), preferred_element_type=F32)
        acc = up[FFN_HALO:FFN_HALO + tm, :] * cw_ref[1:2, :] + cb_ref[...]
        acc = acc + pltpu.roll(up, 1, axis=0)[FFN_HALO:FFN_HALO + tm, :] * cw_ref[0:1, :]
        acc = acc + pltpu.roll(up, rows - 1, axis=0)[FFN_HALO:FFN_HALO + tm, :] * cw_ref[2:3, :]
        return acc

    gate = conv(wg_ref, cwg_ref, cbg_ref)
    val = conv(wv_ref, cwv_ref, cbv_ref)
    o_ref[...] = (_silu(gate) * val).astype(o_ref.dtype)


def _ffn_up(x1, gain, w_up, conv_w, conv_b, *, seq, tm, tn):
    t, d = x1.shape
    f = w_up.shape[1] // 2
    nj = f // tn
    tiles_per_seq = seq // tm
    hb = tm // FFN_HALO
    nhb = t // FFN_HALO
    return pl.pallas_call(
        functools.partial(_ffn_up_kernel, tiles_per_seq=tiles_per_seq),
        grid=(t // tm, nj),
        in_specs=[
            pl.BlockSpec((tm, d), lambda i, j: (i, 0)),
            pl.BlockSpec((FFN_HALO, d), lambda i, j: (jnp.maximum(i * hb - 1, 0), 0)),
            pl.BlockSpec((FFN_HALO, d), lambda i, j: (jnp.minimum((i + 1) * hb, nhb - 1), 0)),
            pl.BlockSpec((1, d), lambda i, j: (0, 0)),
            pl.BlockSpec((d, tn), lambda i, j: (0, j)),
            pl.BlockSpec((d, tn), lambda i, j: (0, j + nj)),
            pl.BlockSpec((3, tn), lambda i, j: (0, j)),
            pl.BlockSpec((3, tn), lambda i, j: (0, j + nj)),
            pl.BlockSpec((1, tn), lambda i, j: (0, j)),
            pl.BlockSpec((1, tn), lambda i, j: (0, j + nj)),
        ],
        out_specs=pl.BlockSpec((tm, tn), lambda i, j: (i, j)),
        out_shape=jax.ShapeDtypeStruct((t, f), BF16),
        scratch_shapes=[pltpu.VMEM((tm + 2 * FFN_HALO, d), BF16)],
        compiler_params=_params(("parallel", "arbitrary")),
        name="ffn_up",
    )(x1, x1, x1, gain, w_up, w_up, conv_w, conv_w, conv_b, conv_b)


def _ffn_down_kernel(x_ref, a_ref, w_ref, g_ref, o_ref):
    v = x_ref[...] + jnp.dot(a_ref[...], w_ref[...], preferred_element_type=F32)
    var = jnp.mean(v * v, axis=-1, keepdims=True)
    o_ref[...] = v * lax.rsqrt(var + NORM_EPS) * g_ref[...]


def _ffn_down(x1, act, w_down, gain, *, tm):
    t, d = x1.shape
    f = act.shape[1]
    return pl.pallas_call(
        _ffn_down_kernel,
        grid=(t // tm,),
        in_specs=[
            pl.BlockSpec((tm, d), lambda i: (i, 0)),
            pl.BlockSpec((tm, f), lambda i: (i, 0)),
            pl.BlockSpec((f, d), lambda i: (0, 0), pipeline_mode=pl.Buffered(1)),
            pl.BlockSpec((1, d), lambda i: (0, 0)),
        ],
        out_specs=pl.BlockSpec((tm, d), lambda i: (i, 0)),
        out_shape=jax.ShapeDtypeStruct((t, d), F32),
        compiler_params=_params(("parallel",)),
        name="ffn_down",
    )(x1, act, w_down, gain)


def _layer(x2, batch, seq, norm_mix_w, w_in, fourier_w, ssm_conv_w, ssm_conv_b, dt_bias_fwd, a_log_fwd,
           dt_bias_bwd, a_log_bwd, ssm_d, ssm_norm_w, w_out, norm_ffn_w, w_up, ffn_conv_w, ffn_conv_b,
           w_down, final_gain, tiles):
    d = x2.shape[1]
    g, r = SSM_GROUPS, HEADS_PER_GROUP
    fw = fourier_w.shape[0] * fourier_w.shape[1]
    sw = g * GROUP_X
    gn = g * SSM_STATE
    main = fw + 2 * sw + 2 * gn
    assert fw == tiles["in_tn"], "u must be exactly the first weight tile for the rotated read"

    wdt = jnp.pad(w_in[:, main:].reshape(d, g, r), ((0, 0), (0, 0), (0, HEAD_ROWS - r)))
    wdt = jnp.pad(wdt.reshape(d, g * HEAD_ROWS), ((0, 0), (0, 128 - g * HEAD_ROWS))).astype(BF16)

    proj, dt_t = _in_proj(x2, norm_mix_w[None, :], w_in, wdt, n=main, nh=g * HEAD_ROWS,
                          tm=tiles["in_tm"], tn=tiles["in_tn"])

    a_out = _fourier_fft(proj, fourier_w, batch=batch, seq=seq, col0=main - fw)

    hp = jnp.stack([dt_bias_fwd, a_log_fwd, dt_bias_bwd, a_log_bwd], axis=-1).reshape(g, r, 4)
    hp = jnp.pad(hp, ((0, 0), (0, HEAD_ROWS - r), (0, 0)))
    d_exp = jnp.repeat(ssm_d, SSM_HEAD_DIM).reshape(g, 1, GROUP_X)
    b_out = _ssd(proj, dt_t, ssm_conv_w, ssm_conv_b[None, :], hp, d_exp, ssm_norm_w.reshape(g, 1, GROUP_X),
                 batch=batch, seq=seq)

    x1 = _out_proj(x2, a_out, b_out, w_out.astype(BF16), tm=tiles["o_tm"], tn=min(tiles["o_tn"], d))

    act = _ffn_up(x1, norm_ffn_w[None, :], w_up, ffn_conv_w, ffn_conv_b[None, :],
                  seq=seq, tm=tiles["u_tm"], tn=tiles["u_tn"])
    return _ffn_down(x1, act, w_down.astype(BF16), final_gain[None, :], tm=tiles["d_tm"])


TILES = dict(in_tm=1024, in_tn=1024, o_tm=512, o_tn=2048, u_tm=1024, u_tn=512, d_tm=256)


def kernel(x, norm_mix_w, w_in, fourier_w, ssm_conv_w, ssm_conv_b, dt_bias_fwd, a_log_fwd, dt_bias_bwd,
           a_log_bwd, ssm_d, ssm_norm_w, w_out, norm_ffn_w, w_up, ffn_conv_w, ffn_conv_b, w_down,
           norm_final_w):
    batch, seq, d = x.shape
    assert norm_mix_w.shape[0] == 1, "one layer"
    out = _layer(x.reshape(batch * seq, d), batch, seq, norm_mix_w[0], w_in[0], fourier_w[0], ssm_conv_w[0],
                 ssm_conv_b[0], dt_bias_fwd[0], a_log_fwd[0], dt_bias_bwd[0], a_log_bwd[0], ssm_d[0],
                 ssm_norm_w[0], w_out[0], norm_ffn_w[0], w_up[0], ffn_conv_w[0], ffn_conv_b[0], w_down[0],
                 norm_final_w, TILES)
    return out.reshape(batch, seq, d)
```

```python
import functools

import jax
import jax.numpy as jnp
import numpy as np
from jax import lax
from jax.experimental import pallas as pl
from jax.experimental.pallas import tpu as pltpu

NORM_EPS = 1e-5
SSM_GROUPS = 8
HEADS_PER_GROUP = 6
HEAD_ROWS = 8
SSM_HEAD_DIM = 64
SSM_STATE = 128
CHUNK = 128
GROUP_X = HEADS_PER_GROUP * SSM_HEAD_DIM
GROUP_XBC = GROUP_X + 2 * SSM_STATE
HALO = 16
VMEM_LIMIT = 56 * 1024 * 1024

F32 = jnp.float32
BF16 = jnp.bfloat16


def _params(semantics):
    return pltpu.CompilerParams(dimension_semantics=semantics, vmem_limit_bytes=VMEM_LIMIT)


def _silu(v):
    return v / (1.0 + jnp.exp(-v))


def _in_proj_kernel(x_ref, g_ref, w_ref, wdt_ref, o_ref, dt_ref, h_ref, *, sub):
    @pl.when(pl.program_id(1) == 0)
    def _():
        def body(r, carry):
            r0 = pl.multiple_of(r * sub, sub)
            xs = x_ref[pl.ds(r0, sub), :]
            var = jnp.mean(xs * xs, axis=-1, keepdims=True)
            hs = xs * lax.rsqrt(var + NORM_EPS) * g_ref[...]
            h_ref[pl.ds(r0, sub), :] = hs.astype(BF16)
            return carry
        lax.fori_loop(0, x_ref.shape[0] // sub, body, 0)
        dt = jnp.dot(h_ref[...], wdt_ref[...], preferred_element_type=F32)
        lanes = wdt_ref.shape[1]
        for r in range(x_ref.shape[0] // lanes):
            dt_ref[:, r * lanes:(r + 1) * lanes] = jnp.transpose(
                dt[r * lanes:(r + 1) * lanes, :])[:dt_ref.shape[0], :]

    o_ref[...] = jnp.dot(h_ref[...], w_ref[...], preferred_element_type=F32).astype(o_ref.dtype)


def _in_proj(x2, gain, w, wdt, *, n, nh, tm, tn):
    t, d = x2.shape
    nj = n // tn
    return pl.pallas_call(
        functools.partial(_in_proj_kernel, sub=128),
        grid=(t // tm, nj),
        in_specs=[
            pl.BlockSpec((tm, d), lambda i, j: (i, 0)),
            pl.BlockSpec((1, d), lambda i, j: (0, 0)),
            pl.BlockSpec((d, tn), lambda i, j: (0, (j + 1) % nj)),
            pl.BlockSpec(wdt.shape, lambda i, j: (0, 0)),
        ],
        out_specs=[
            pl.BlockSpec((tm, tn), lambda i, j: (i, j)),
            pl.BlockSpec((nh, tm), lambda i, j: (0, i)),
        ],
        out_shape=[jax.ShapeDtypeStruct((t, n), BF16), jax.ShapeDtypeStruct((nh, t), F32)],
        scratch_shapes=[pltpu.VMEM((tm, d), BF16)],
        compiler_params=_params(("parallel", "arbitrary")),
        name="in_proj",
    )(x2, gain, w, wdt)


FFT_SPLIT = 4
FFT_COLS = 256


def _fourier_fft_kernel(u_ref, cc_ref, sc_ref, w_ref, l2_ref, twc_ref, tws_ref, o_ref, z_ref, o32_ref, *, scale):
    n2 = l2_ref.shape[0]
    c = cc_ref.shape[0]
    groups = FFT_COLS // c
    cb = pl.program_id(1)
    for gi in range(groups):
        w = w_ref[cb * groups + gi]
        mix_c = (jnp.dot(cc_ref[...], w, preferred_element_type=F32) * scale).astype(BF16)
        mix_s = (jnp.dot(sc_ref[...], w, preferred_element_type=F32) * scale).astype(BF16)
        ug = u_ref[:, gi * c:(gi + 1) * c]
        z_ref[0, :, gi * c:(gi + 1) * c] = jnp.dot(ug, mix_c, preferred_element_type=F32)
        z_ref[1, :, gi * c:(gi + 1) * c] = -jnp.dot(ug, mix_s, preferred_element_type=F32)

    zc = [z_ref[0, a * n2:(a + 1) * n2, :] for a in range(FFT_SPLIT)]
    zs = [z_ref[1, a * n2:(a + 1) * n2, :] for a in range(FFT_SPLIT)]
    ec, es, fc, fs = zc[0] + zc[2], zs[0] + zs[2], zc[0] - zc[2], zs[0] - zs[2]
    gc, gs, hc, hs = zc[1] + zc[3], zs[1] + zs[3], zc[1] - zc[3], zs[1] - zs[3]
    butterflies = [(ec + gc, es + gs), (fc + hs, fs - hc), (ec - gc, es - gs), (fc - hs, fs + hc)]
    lane_reps = FFT_COLS // twc_ref.shape[2]
    for k_lo, (yc, ys) in enumerate(butterflies):
        if k_lo > 0:
            twc = jnp.concatenate([twc_ref[k_lo - 1]] * lane_reps, axis=1)
            tws = jnp.concatenate([tws_ref[k_lo - 1]] * lane_reps, axis=1)
            yc, ys = yc * twc + ys * tws, ys * twc - yc * tws
        rhs = jnp.concatenate([yc, ys], axis=0).astype(BF16)
        x = jnp.dot(l2_ref[...], rhs, preferred_element_type=F32)
        for h in range(FFT_COLS // 128):
            o32_ref[h, pl.ds(k_lo, n2, stride=FFT_SPLIT), :] = x[:, 128 * h:128 * (h + 1)]
    for h in range(FFT_COLS // 128):
        o_ref[:, 128 * h:128 * (h + 1)] = o32_ref[h].astype(o_ref.dtype)


def _fourier_fft(proj, w_mix, *, batch, seq, col0):
    g, c, _ = w_mix.shape
    n2 = seq // FFT_SPLIT
    assert seq % FFT_SPLIT == 0 and FFT_COLS % c == 0 and (g * c) % FFT_COLS == 0 and col0 % FFT_COLS == 0
    blk0 = col0 // FFT_COLS
    scale = float(1.0 / np.sqrt(seq * c))
    cos_c, sin_c = _dft_tables(c, 8)
    cos_n, sin_n = _dft_tables(n2, 32)
    l2 = jnp.concatenate([cos_n, sin_n], axis=1).astype(BF16)
    k_lo = jnp.arange(1, FFT_SPLIT, dtype=jnp.int32)[:, None]
    ang = (k_lo * jnp.arange(n2, dtype=jnp.int32)[None, :]).astype(F32) * (2.0 * np.pi / seq)
    twc = jnp.broadcast_to(jnp.cos(ang)[:, :, None], (FFT_SPLIT - 1, n2, 128))
    tws = jnp.broadcast_to(jnp.sin(ang)[:, :, None], (FFT_SPLIT - 1, n2, 128))
    const2 = lambda b, j: (0, 0)
    const3 = lambda b, j: (0, 0, 0)
    return pl.pallas_call(
        functools.partial(_fourier_fft_kernel, scale=scale),
        grid=(batch, g * c // FFT_COLS),
        in_specs=[
            pl.BlockSpec((seq, FFT_COLS), lambda b, j: (b, blk0 + j)),
            pl.BlockSpec((c, c), const2),
            pl.BlockSpec((c, c), const2),
            pl.BlockSpec((g, c, c), const3),
            pl.BlockSpec(l2.shape, const2),
            pl.BlockSpec(twc.shape, const3),
            pl.BlockSpec(tws.shape, const3),
        ],
        out_specs=pl.BlockSpec((seq, FFT_COLS), lambda b, j: (b, j)),
        out_shape=jax.ShapeDtypeStruct((batch * seq, g * c), BF16),
        scratch_shapes=[pltpu.VMEM((2, seq, FFT_COLS), F32),
                        pltpu.VMEM((FFT_COLS // 128, seq, 128), F32)],
        compiler_params=_params(("parallel", "arbitrary")),
        name="fourier",
    )(proj, cos_c.astype(BF16), sin_c.astype(BF16), w_mix.astype(BF16), l2, twc, tws)


def _dft_tables(n, split):
    hi = n // split
    s = jnp.arange(n, dtype=jnp.int32)[None, :]
    k1 = jnp.arange(hi, dtype=jnp.int32)[:, None]
    k0 = jnp.arange(split, dtype=jnp.int32)[:, None]
    a1 = ((k1 * s * split) % n).astype(F32) * (2.0 * np.pi / n)
    a0 = ((k0 * s) % n).astype(F32) * (2.0 * np.pi / n)
    c1, s1 = jnp.cos(a1)[:, None, :], jnp.sin(a1)[:, None, :]
    c0, s0 = jnp.cos(a0)[None, :, :], jnp.sin(a0)[None, :, :]
    cos_m = (c1 * c0 - s1 * s0).reshape(n, n)
    sin_m = (s1 * c0 + c1 * s0).reshape(n, n)
    return cos_m, sin_m


PIECES_P = 3
COL_DIR = PIECES_P * HEAD_ROWS
SEL_COLS = HEADS_PER_GROUP * CHUNK
ROW_Q, ROW_YS, ROW_W, ROW_CD = 0, 1, 2, 3
ROW_TABLES = 4
LOG2E = float(np.log2(np.e))
CONV_TAPS = 5
WIN_ROWS = CHUNK + 2 * HALO


def _selection_matrices():
    sel = np.zeros((2, CHUNK, SEL_COLS), np.float32)
    for d in range(2):
        for r in range(HEADS_PER_GROUP):
            for piece in range(PIECES_P):
                sel[d, d * COL_DIR + piece * HEAD_ROWS + r, r * CHUNK:(r + 1) * CHUNK] = 1.0
    return sel


def _shift_matrix():
    taps = [k for k in range(CONV_TAPS) if k != CONV_TAPS // 2]
    m = np.zeros((len(taps) * CHUNK, WIN_ROWS), np.float32)
    for i, k in enumerate(taps):
        for t in range(CHUNK):
            m[i * CHUNK + t, HALO + t + k - CONV_TAPS // 2] = 1.0
    return m


def _softplus(v):
    return jnp.maximum(v, 0.0) + jnp.log1p(jnp.exp(-jnp.abs(v)))


def _bf16_pieces(v, n):
    out = []
    for _ in range(n - 1):
        hi = v.astype(BF16).astype(F32)
        out.append(hi)
        v = v - hi
    out.append(v)
    return out


def _ssd_kernel(z_ref, x_ref, b_ref, c_ref, dt_ref, cwx_ref, cwb_ref, cwc_ref, cbx_ref, cbb_ref, cbc_ref,
                hp_ref, d_ref, nw_ref, sel_ref, shift_ref, o_ref,
                xc_ref, xt_ref, colq_ref, rowq_ref, y_ref, st_ref, pb_ref, cbm_ref):
    seq = x_ref.shape[0]
    nchunks = seq // CHUNK
    half = nchunks // 2
    xbc_rows = lambda start, n: jnp.concatenate(
        [ref[pl.ds(start, n), :] for ref in (x_ref, b_ref, c_ref)], axis=1)
    conv_w = jnp.concatenate([cwx_ref[...], cwb_ref[...], cwc_ref[...]], axis=1)
    conv_b = jnp.concatenate([cbx_ref[...], cbb_ref[...], cbc_ref[...]], axis=1)
    li = lax.broadcasted_iota(jnp.int32, (CHUNK, CHUNK), 0)
    si = lax.broadcasted_iota(jnp.int32, (CHUNK, CHUNK), 1)
    upper_b = jnp.where(li <= si, 1.0, 0.0).astype(BF16)
    lane_lo = si < SSM_HEAD_DIM
    lane_lo_row = lax.broadcasted_iota(jnp.int32, (1, 128), 1) < SSM_HEAD_DIM

    hp = hp_ref[...]
    bias_f, a_f = hp[:, 0:1], -jnp.exp(hp[:, 1:2])
    bias_b, a_b = hp[:, 2:3], -jnp.exp(hp[:, 3:4])

    def prep(c):
        r0 = pl.multiple_of(c * CHUNK, CHUNK)
        cur = xbc_rows(r0, CHUNK)
        p0 = pl.multiple_of(jnp.maximum(r0 - HALO, 0), HALO)
        n0 = pl.multiple_of(jnp.minimum(r0 + CHUNK, seq - HALO), HALO)
        prev = xbc_rows(p0, HALO)
        nxt = xbc_rows(n0, HALO)
        prev = jnp.where(c > 0, prev, jnp.zeros_like(prev))
        nxt = jnp.where(c < nchunks - 1, nxt, jnp.zeros_like(nxt))
        win = jnp.concatenate([prev, cur, nxt], axis=0)
        taps = jnp.dot(shift_ref[...], win, preferred_element_type=F32)
        mid = CONV_TAPS // 2
        acc = cur.astype(F32) * conv_w[mid:mid + 1, :] + conv_b
        for i, k in enumerate([k for k in range(CONV_TAPS) if k != mid]):
            acc = acc + taps[i * CHUNK:(i + 1) * CHUNK, :] * conv_w[k:k + 1, :]
        act = _silu(acc)
        xc_ref[pl.ds(r0, CHUNK), :] = act.astype(BF16)
        xt_ref[:, pl.ds(r0, CHUNK)] = jnp.concatenate(
            [jnp.transpose(act[:, 128 * j:128 * (j + 1)]) for j in range(GROUP_X // 128)], axis=0).astype(BF16)

    prep_unroll = min(8, nchunks)

    def prep_group(i, carry):
        for u in range(prep_unroll):
            prep(prep_unroll * i + u)
        return carry

    lax.fori_loop(0, nchunks // prep_unroll, prep_group, 0)

    trows = nchunks * HEAD_ROWS
    dt_all = dt_ref[...]
    dtr = jnp.concatenate([dt_all[:, c * CHUNK:(c + 1) * CHUNK] for c in range(nchunks)], axis=0)
    per_row = lambda col: jnp.concatenate([jnp.broadcast_to(col, (HEAD_ROWS, CHUNK))] * nchunks, axis=0)
    lane_const = lambda v: jnp.broadcast_to(v, (trows, CHUNK))

    def cumsum(v):
        stacked = jnp.concatenate(_bf16_pieces(v, 3), axis=0).astype(BF16)
        out = jnp.dot(stacked, upper_b, preferred_element_type=F32)
        return (out[:trows] + out[trows:2 * trows]) + out[2 * trows:]

    dt_f = _softplus(dtr + per_row(bias_f))
    dt_b = _softplus(dtr + per_row(bias_b))
    adt_f = dt_f * per_row(a_f)
    adt_b = dt_b * per_row(a_b)
    cum_f = cumsum(adt_f)
    cum_b = cumsum(adt_b)
    tot_f = cum_f[:, CHUNK - 1:CHUNK]
    tot_b = cum_b[:, CHUNK - 1:CHUNK]
    ex_b = cum_b - adt_b
    p2_f = cum_f * LOG2E
    p2_b = -ex_b * LOG2E
    tables = [p2_f - jnp.log(dt_f) * LOG2E, jnp.zeros((trows, CHUNK), F32),
              jnp.exp(tot_f - cum_f) * dt_f, lane_const(jnp.exp(tot_f)),
              p2_b - jnp.log(dt_b) * LOG2E, lane_const(tot_b * LOG2E),
              jnp.exp(ex_b) * dt_b, lane_const(jnp.exp(tot_b))]
    for q, tab in enumerate(tables):
        rowq_ref[q] = tab
    pieces = _bf16_pieces(p2_f, PIECES_P) + _bf16_pieces(p2_b, PIECES_P)
    pad = jnp.zeros((CHUNK - 2 * COL_DIR, CHUNK), F32)
    for c in range(nchunks):
        rows = [p[c * HEAD_ROWS:(c + 1) * HEAD_ROWS, :] for p in pieces] + [pad]
        colq_ref[c * CHUNK:(c + 1) * CHUNK, :] = jnp.transpose(jnp.concatenate(rows, axis=0)).astype(BF16)

    nt = (((1,), (1,)), ((), ()))

    def stage_ahead(c, direction):
        r0 = pl.multiple_of(c * CHUNK, CHUNK)
        pb_ref[direction] = jnp.dot(colq_ref[pl.ds(r0, CHUNK), :], sel_ref[direction],
                                    preferred_element_type=F32)
        cbm_ref[direction] = lax.dot_general(
            xc_ref[pl.ds(r0, CHUNK), GROUP_X + SSM_STATE:], xc_ref[pl.ds(r0, CHUNK), GROUP_X:GROUP_X + SSM_STATE],
            nt, preferred_element_type=F32)

    def chunk(c, direction):
        r0 = pl.multiple_of(c * CHUNK, CHUNK)
        xcb = xc_ref[pl.ds(r0, CHUNK), :]
        xs = xcb[:, :GROUP_X]
        bm = xcb[:, GROUP_X:GROUP_X + SSM_STATE]
        cm = xcb[:, GROUP_X + SSM_STATE:]
        r8 = pl.multiple_of(c * HEAD_ROWS, HEAD_ROWS)
        tabs = [rowq_ref[direction * ROW_TABLES + q, pl.ds(r8, HEAD_ROWS), :] for q in range(ROW_TABLES)]
        row = lambda q, r: tabs[q][r:r + 1, :]
        pb = pb_ref[direction]
        cb = cbm_ref[direction]
        step = 1 if direction == 0 else -1
        stage_ahead(jnp.clip(c + step, 0, nchunks - 1), direction)
        mask = (li >= si) if direction == 0 else (si >= li)

        state_t = st_ref[direction]
        y_off = lax.dot_general(cm, state_t.astype(BF16), nt, preferred_element_type=F32)

        xs_lo = jnp.where(lane_lo_row, 1.0, 0.0).astype(BF16)
        xs_hi = jnp.where(lane_lo_row, 0.0, 1.0).astype(BF16)
        tiles = []
        for j in range(HEADS_PER_GROUP // 2):
            xp = xs[:, 128 * j:128 * (j + 1)]
            pa = pb[:, 2 * j * CHUNK:(2 * j + 1) * CHUNK]
            pc = pb[:, (2 * j + 1) * CHUNK:(2 * j + 2) * CHUNK]
            ys = jnp.exp2(jnp.where(lane_lo, pa, pc)
                          + jnp.where(lane_lo_row, row(ROW_YS, 2 * j), row(ROW_YS, 2 * j + 1)))
            ma = (cb * jnp.exp2(jnp.where(mask, pa - row(ROW_Q, 2 * j), -jnp.inf))).astype(BF16)
            mc = (cb * jnp.exp2(jnp.where(mask, pc - row(ROW_Q, 2 * j + 1), -jnp.inf))).astype(BF16)
            tiles.append(y_off[:, 128 * j:128 * (j + 1)] * ys
                         + jnp.dot(ma, xp * xs_lo, preferred_element_type=F32)
                         + jnp.dot(mc, xp * xs_hi, preferred_element_type=F32))
        y = jnp.concatenate(tiles, axis=1)

        per_head = lambda off, dtype=F32: jnp.concatenate(
            [jnp.broadcast_to(row(off, r).astype(dtype), (SSM_HEAD_DIM, CHUNK)) for r in range(HEADS_PER_GROUP)],
            axis=0)
        xw = xt_ref[:, pl.ds(r0, CHUNK)] * per_head(ROW_W, BF16)
        new = jnp.dot(xw, bm, preferred_element_type=F32)
        st_ref[direction] = state_t * per_head(ROW_CD) + new
        return r0, xs, y

    def finish(r0, xs, y):
        y = y + y_ref[pl.ds(r0, CHUNK), :] + d_ref[...] * xs.astype(F32)
        y = y * _silu(z_ref[pl.ds(r0, CHUNK), :].astype(F32))
        ms = jnp.mean(y * y, axis=-1, keepdims=True)
        o_ref[pl.ds(r0, CHUNK), :] = (y * lax.rsqrt(ms + NORM_EPS) * nw_ref[...]).astype(o_ref.dtype)

    st_ref[...] = jnp.zeros_like(st_ref)

    scan_unroll = 4 if half % 4 == 0 else (2 if half % 2 == 0 else 1)

    def first_visits(t, carry):
        for u in range(scan_unroll):
            i = scan_unroll * t + u
            for c, direction in ((i, 0), (nchunks - 1 - i, 1)):
                r0, _, y = chunk(c, direction)
                y_ref[pl.ds(r0, CHUNK), :] = y
        return carry

    def second_visits(t, carry):
        for u in range(scan_unroll):
            i = half + scan_unroll * t + u
            for c, direction in ((i, 0), (nchunks - 1 - i, 1)):
                finish(*chunk(c, direction))
        return carry

    stage_ahead(0, 0)
    stage_ahead(nchunks - 1, 1)
    lax.fori_loop(0, half // scan_unroll, first_visits, 0)
    lax.fori_loop(0, half // scan_unroll, second_visits, 0)


def _ssd(proj, dt_t, conv_w, conv_b, head_params, d_exp, norm_w, *, batch, seq):
    g = SSM_GROUPS
    width = conv_w.shape[0]
    assert width == CONV_TAPS and (seq // CHUNK) % 2 == 0
    sel = jnp.asarray(_selection_matrices(), BF16)
    shift = jnp.asarray(_shift_matrix(), BF16)
    x_blk = g
    b_blk = 2 * g * GROUP_X // SSM_STATE
    c_blk = b_blk + g
    cwb_blk = g * GROUP_X // SSM_STATE
    cwc_blk = cwb_blk + g
    return pl.pallas_call(
        _ssd_kernel,
        grid=(batch, g),
        in_specs=[
            pl.BlockSpec((seq, GROUP_X), lambda b, k: (b, k)),
            pl.BlockSpec((seq, GROUP_X), lambda b, k: (b, x_blk + k)),
            pl.BlockSpec((seq, SSM_STATE), lambda b, k: (b, b_blk + k)),
            pl.BlockSpec((seq, SSM_STATE), lambda b, k: (b, c_blk + k)),
            pl.BlockSpec((HEAD_ROWS, seq), lambda b, k: (k, b)),
            pl.BlockSpec((width, GROUP_X), lambda b, k: (0, k)),
            pl.BlockSpec((width, SSM_STATE), lambda b, k: (0, cwb_blk + k)),
            pl.BlockSpec((width, SSM_STATE), lambda b, k: (0, cwc_blk + k)),
            pl.BlockSpec((1, GROUP_X), lambda b, k: (0, k)),
            pl.BlockSpec((1, SSM_STATE), lambda b, k: (0, cwb_blk + k)),
            pl.BlockSpec((1, SSM_STATE), lambda b, k: (0, cwc_blk + k)),
            pl.BlockSpec((None, HEAD_ROWS, 4), lambda b, k: (k, 0, 0)),
            pl.BlockSpec((None, 1, GROUP_X), lambda b, k: (k, 0, 0)),
            pl.BlockSpec((None, 1, GROUP_X), lambda b, k: (k, 0, 0)),
            pl.BlockSpec(sel.shape, lambda b, k: (0, 0, 0)),
            pl.BlockSpec(shift.shape, lambda b, k: (0, 0)),
        ],
        out_specs=pl.BlockSpec((seq, GROUP_X), lambda b, k: (b, k)),
        out_shape=jax.ShapeDtypeStruct((batch * seq, g * GROUP_X), BF16),
        scratch_shapes=[
            pltpu.VMEM((seq, GROUP_XBC), BF16),
            pltpu.VMEM((GROUP_X, seq), BF16),
            pltpu.VMEM((seq, CHUNK), BF16),
            pltpu.VMEM((2 * ROW_TABLES, seq // CHUNK * HEAD_ROWS, CHUNK), F32),
            pltpu.VMEM((seq, GROUP_X), F32),
            pltpu.VMEM((2, GROUP_X, SSM_STATE), F32),
            pltpu.VMEM((2, CHUNK, SEL_COLS), F32),
            pltpu.VMEM((2, CHUNK, CHUNK), F32),
        ],
        compiler_params=_params(("parallel", "arbitrary")),
        name="ssd",
    )(proj, proj, proj, proj, dt_t, conv_w, conv_w, conv_w, conv_b, conv_b, conv_b,
      head_params, d_exp, norm_w, sel, shift)


def _out_proj_kernel(x_ref, a_ref, b_ref, w_ref, o_ref):
    ka = a_ref.shape[1]
    acc = jnp.dot(a_ref[...], w_ref[:ka, :], preferred_element_type=F32)
    acc = acc + jnp.dot(b_ref[...], w_ref[ka:, :], preferred_element_type=F32)
    o_ref[...] = x_ref[...] + acc


def _out_proj(x2, a_out, b_out, w, *, tm, tn):
    t, d = x2.shape
    ka, kb = a_out.shape[1], b_out.shape[1]
    resident = pl.Buffered(1) if tn == d else None
    return pl.pallas_call(
        _out_proj_kernel,
        grid=(t // tm, d // tn),
        in_specs=[
            pl.BlockSpec((tm, tn), lambda i, j: (i, j)),
            pl.BlockSpec((tm, ka), lambda i, j: (i, 0)),
            pl.BlockSpec((tm, kb), lambda i, j: (i, 0)),
            pl.BlockSpec((ka + kb, tn), lambda i, j: (0, j), pipeline_mode=resident),
        ],
        out_specs=pl.BlockSpec((tm, tn), lambda i, j: (i, j)),
        out_shape=jax.ShapeDtypeStruct((t, d), F32),
        compiler_params=_params(("parallel", "arbitrary")),
        name="out_proj",
    )(x2, a_out, b_out, w)


FFN_HALO = HALO


def _ffn_up_kernel(x_ref, xp_ref, xn_ref, g_ref, wg_ref, wv_ref, cwg_ref, cwv_ref, cbg_ref, cbv_ref,
                   o_ref, h_ref, *, tiles_per_seq):
    tm = x_ref.shape[0]
    i = pl.program_id(0)

    def norm(v):
        var = jnp.mean(v * v, axis=-1, keepdims=True)
        return v * lax.rsqrt(var + NORM_EPS) * g_ref[...]

    @pl.when(pl.program_id(1) == 0)
    def _():
        first = (i % tiles_per_seq) == 0
        last = (i % tiles_per_seq) == tiles_per_seq - 1
        h_ref[0:FFN_HALO, :] = jnp.where(first, 0.0, norm(xp_ref[...])).astype(BF16)
        h_ref[FFN_HALO + tm:, :] = jnp.where(last, 0.0, norm(xn_ref[...])).astype(BF16)

        def body(r, carry):
            r0 = pl.multiple_of(r * 128, 128)
            h_ref[pl.ds(FFN_HALO + r0, 128), :] = norm(x_ref[pl.ds(r0, 128), :]).astype(BF16)
            return carry
        lax.fori_loop(0, tm // 128, body, 0)

    rows = tm + 2 * FFN_HALO

    def conv(w_ref, cw_ref, cb_ref):
        up = jnp.dot(h_ref[...], w_ref[...].astype(BF16), preferred_element_type=F32)
        acc = up[FFN_HALO:FFN_HALO + tm, :] * cw_ref[1:2, :] + cb_ref[...]
        acc = acc + pltpu.roll(up, 1, axis=0)[FFN_HALO:FFN_HALO + tm, :] * cw_ref[0:1, :]
        acc = acc + pltpu.roll(up, rows - 1, axis=0)[FFN_HALO:FFN_HALO + tm, :] * cw_ref[2:3, :]
        return acc

    gate = conv(wg_ref, cwg_ref, cbg_ref)
    val = conv(wv_ref, cwv_ref, cbv_ref)
    o_ref[...] = (_silu(gate) * val).astype(o_ref.dtype)


def _ffn_up(x1, gain, w_up, conv_w, conv_b, *, seq, tm, tn):
    t, d = x1.shape
    f = w_up.shape[1] // 2
    nj = f // tn
    tiles_per_seq = seq // tm
    hb = tm // FFN_HALO
    nhb = t // FFN_HALO
    return pl.pallas_call(
        functools.partial(_ffn_up_kernel, tiles_per_seq=tiles_per_seq),
        grid=(t // tm, nj),
        in_specs=[
            pl.BlockSpec((tm, d), lambda i, j: (i, 0)),
            pl.BlockSpec((FFN_HALO, d), lambda i, j: (jnp.maximum(i * hb - 1, 0), 0)),
            pl.BlockSpec((FFN_HALO, d), lambda i, j: (jnp.minimum((i + 1) * hb, nhb - 1), 0)),
            pl.BlockSpec((1, d), lambda i, j: (0, 0)),
            pl.BlockSpec((d, tn), lambda i, j: (0, j)),
            pl.BlockSpec((d, tn), lambda i, j: (0, j + nj)),
            pl.BlockSpec((3, tn), lambda i, j: (0, j)),
            pl.BlockSpec((3, tn), lambda i, j: (0, j + nj)),
            pl.BlockSpec((1, tn), lambda i, j: (0, j)),
            pl.BlockSpec((1, tn), lambda i, j: (0, j + nj)),
        ],
        out_specs=pl.BlockSpec((tm, tn), lambda i, j: (i, j)),
        out_shape=jax.ShapeDtypeStruct((t, f), BF16),
        scratch_shapes=[pltpu.VMEM((tm + 2 * FFN_HALO, d), BF16)],
        compiler_params=_params(("parallel", "arbitrary")),
        name="ffn_up",
    )(x1, x1, x1, gain, w_up, w_up, conv_w, conv_w, conv_b, conv_b)


def _ffn_down_kernel(x_ref, a_ref, w_ref, g_ref, o_ref):
    v = x_ref[...] + jnp.dot(a_ref[...], w_ref[...], preferred_element_type=F32)
    var = jnp.mean(v * v, axis=-1, keepdims=True)
    o_ref[...] = v * lax.rsqrt(var + NORM_EPS) * g_ref[...]


def _ffn_down(x1, act, w_down, gain, *, tm):
    t, d = x1.shape
    f = act.shape[1]
    return pl.pallas_call(
        _ffn_down_kernel,
        grid=(t // tm,),
        in_specs=[
            pl.BlockSpec((tm, d), lambda i: (i, 0)),
            pl.BlockSpec((tm, f), lambda i: (i, 0)),
            pl.BlockSpec((f, d), lambda i: (0, 0), pipeline_mode=pl.Buffered(1)),
            pl.BlockSpec((1, d), lambda i: (0, 0)),
        ],
        out_specs=pl.BlockSpec((tm, d), lambda i: (i, 0)),
        out_shape=jax.ShapeDtypeStruct((t, d), F32),
        compiler_params=_params(("parallel",)),
        name="ffn_down",
    )(x1, act, w_down, gain)


def _layer(x2, batch, seq, norm_mix_w, w_in, fourier_w, ssm_conv_w, ssm_conv_b, dt_bias_fwd, a_log_fwd,
           dt_bias_bwd, a_log_bwd, ssm_d, ssm_norm_w, w_out, norm_ffn_w, w_up, ffn_conv_w, ffn_conv_b,
           w_down, final_gain, tiles):
    d = x2.shape[1]
    g, r = SSM_GROUPS, HEADS_PER_GROUP
    fw = fourier_w.shape[0] * fourier_w.shape[1]
    sw = g * GROUP_X
    gn = g * SSM_STATE
    main = fw + 2 * sw + 2 * gn
    assert fw == tiles["in_tn"], "u must be exactly the first weight tile for the rotated read"

    wdt = jnp.pad(w_in[:, main:].reshape(d, g, r), ((0, 0), (0, 0), (0, HEAD_ROWS - r)))
    wdt = jnp.pad(wdt.reshape(d, g * HEAD_ROWS), ((0, 0), (0, 128 - g * HEAD_ROWS))).astype(BF16)

    proj, dt_t = _in_proj(x2, norm_mix_w[None, :], w_in.astype(BF16), wdt, n=main, nh=g * HEAD_ROWS,
                          tm=tiles["in_tm"], tn=tiles["in_tn"])

    a_out = _fourier_fft(proj, fourier_w, batch=batch, seq=seq, col0=main - fw)

    hp = jnp.stack([dt_bias_fwd, a_log_fwd, dt_bias_bwd, a_log_bwd], axis=-1).reshape(g, r, 4)
    hp = jnp.pad(hp, ((0, 0), (0, HEAD_ROWS - r), (0, 0)))
    d_exp = jnp.repeat(ssm_d, SSM_HEAD_DIM).reshape(g, 1, GROUP_X)
    b_out = _ssd(proj, dt_t, ssm_conv_w, ssm_conv_b[None, :], hp, d_exp, ssm_norm_w.reshape(g, 1, GROUP_X),
                 batch=batch, seq=seq)

    x1 = _out_proj(x2, a_out, b_out, w_out.astype(BF16), tm=tiles["o_tm"], tn=min(tiles["o_tn"], d))

    act = _ffn_up(x1, norm_ffn_w[None, :], w_up, ffn_conv_w, ffn_conv_b[None, :],
                  seq=seq, tm=tiles["u_tm"], tn=tiles["u_tn"])
    return _ffn_down(x1, act, w_down.astype(BF16), final_gain[None, :], tm=tiles["d_tm"])


TILES = dict(in_tm=1024, in_tn=1024, o_tm=512, o_tn=2048, u_tm=1024, u_tn=512, d_tm=256)


def kernel(x, norm_mix_w, w_in, fourier_w, ssm_conv_w, ssm_conv_b, dt_bias_fwd, a_log_fwd, dt_bias_bwd,
           a_log_bwd, ssm_d, ssm_norm_w, w_out, norm_ffn_w, w_up, ffn_conv_w, ffn_conv_b, w_down,
           norm_final_w):
    batch, seq, d = x.shape
    assert norm_mix_w.shape[0] == 1, "one layer"
    out = _layer(x.reshape(batch * seq, d), batch, seq, norm_mix_w[0], w_in[0], fourier_w[0], ssm_conv_w[0],
                 ssm_conv_b[0], dt_bias_fwd[0], a_log_fwd[0], dt_bias_bwd[0], a_log_bwd[0], ssm_d[0],
                 ssm_norm_w[0], w_out[0], norm_ffn_w[0], w_up[0], ffn_conv_w[0], ffn_conv_b[0], w_down[0],
                 norm_final_w, TILES)
    return out.reshape(batch, seq, d)
```

```python
import functools

import jax
import jax.numpy as jnp
import numpy as np
from jax import lax
from jax.experimental import pallas as pl
from jax.experimental.pallas import tpu as pltpu

NORM_EPS = 1e-5
SSM_GROUPS = 8
HEADS_PER_GROUP = 6
HEAD_ROWS = 8
SSM_HEAD_DIM = 64
SSM_STATE = 128
CHUNK = 128
GROUP_X = HEADS_PER_GROUP * SSM_HEAD_DIM
GROUP_XBC = GROUP_X + 2 * SSM_STATE
HALO = 16
VMEM_LIMIT = 56 * 1024 * 1024

F32 = jnp.float32
BF16 = jnp.bfloat16


def _params(semantics):
    return pltpu.CompilerParams(dimension_semantics=semantics, vmem_limit_bytes=VMEM_LIMIT)


def _silu(v):
    return v / (1.0 + jnp.exp(-v))


def _in_proj_kernel(x_ref, g_ref, w_ref, wdt_ref, o_ref, dt_ref, h_ref, *, sub):
    @pl.when(pl.program_id(1) == 0)
    def _():
        def body(r, carry):
            r0 = pl.multiple_of(r * sub, sub)
            xs = x_ref[pl.ds(r0, sub), :]
            var = jnp.mean(xs * xs, axis=-1, keepdims=True)
            hs = xs * lax.rsqrt(var + NORM_EPS) * g_ref[...]
            h_ref[pl.ds(r0, sub), :] = hs.astype(BF16)
            return carry
        lax.fori_loop(0, x_ref.shape[0] // sub, body, 0)
        dt = jnp.dot(h_ref[...], wdt_ref[...], preferred_element_type=F32)
        lanes = wdt_ref.shape[1]
        for r in range(x_ref.shape[0] // lanes):
            dt_ref[:, r * lanes:(r + 1) * lanes] = jnp.transpose(
                dt[r * lanes:(r + 1) * lanes, :])[:dt_ref.shape[0], :]

    o_ref[...] = jnp.dot(h_ref[...], w_ref[...], preferred_element_type=F32).astype(o_ref.dtype)


def _in_proj(x2, gain, w, wdt, *, n, nh, tm, tn):
    t, d = x2.shape
    nj = n // tn
    return pl.pallas_call(
        functools.partial(_in_proj_kernel, sub=128),
        grid=(t // tm, nj),
        in_specs=[
            pl.BlockSpec((tm, d), lambda i, j: (i, 0)),
            pl.BlockSpec((1, d), lambda i, j: (0, 0)),
            pl.BlockSpec((d, tn), lambda i, j: (0, (j + 1) % nj)),
            pl.BlockSpec(wdt.shape, lambda i, j: (0, 0)),
        ],
        out_specs=[
            pl.BlockSpec((tm, tn), lambda i, j: (i, j)),
            pl.BlockSpec((nh, tm), lambda i, j: (0, i)),
        ],
        out_shape=[jax.ShapeDtypeStruct((t, n), BF16), jax.ShapeDtypeStruct((nh, t), F32)],
        scratch_shapes=[pltpu.VMEM((tm, d), BF16)],
        compiler_params=_params(("parallel", "arbitrary")),
        name="in_proj",
    )(x2, gain, w, wdt)


FFT_SPLIT = 4
FFT_COLS = 256


def _fourier_fft_kernel(u_ref, cc_ref, sc_ref, w_ref, l2_ref, twc_ref, tws_ref, o_ref, z_ref, o32_ref, *, scale):
    n2 = l2_ref.shape[0]
    c = cc_ref.shape[0]
    groups = FFT_COLS // c
    cb = pl.program_id(1)
    for gi in range(groups):
        w = w_ref[cb * groups + gi]
        mix_c = (jnp.dot(cc_ref[...], w, preferred_element_type=F32) * scale).astype(BF16)
        mix_s = (jnp.dot(sc_ref[...], w, preferred_element_type=F32) * scale).astype(BF16)
        ug = u_ref[:, gi * c:(gi + 1) * c]
        z_ref[0, :, gi * c:(gi + 1) * c] = jnp.dot(ug, mix_c, preferred_element_type=F32)
        z_ref[1, :, gi * c:(gi + 1) * c] = -jnp.dot(ug, mix_s, preferred_element_type=F32)

    zc = [z_ref[0, a * n2:(a + 1) * n2, :] for a in range(FFT_SPLIT)]
    zs = [z_ref[1, a * n2:(a + 1) * n2, :] for a in range(FFT_SPLIT)]
    ec, es, fc, fs = zc[0] + zc[2], zs[0] + zs[2], zc[0] - zc[2], zs[0] - zs[2]
    gc, gs, hc, hs = zc[1] + zc[3], zs[1] + zs[3], zc[1] - zc[3], zs[1] - zs[3]
    butterflies = [(ec + gc, es + gs), (fc + hs, fs - hc), (ec - gc, es - gs), (fc - hs, fs + hc)]
    lane_reps = FFT_COLS // twc_ref.shape[2]
    for k_lo, (yc, ys) in enumerate(butterflies):
        if k_lo > 0:
            twc = jnp.concatenate([twc_ref[k_lo - 1]] * lane_reps, axis=1)
            tws = jnp.concatenate([tws_ref[k_lo - 1]] * lane_reps, axis=1)
            yc, ys = yc * twc + ys * tws, ys * twc - yc * tws
        rhs = jnp.concatenate([yc, ys], axis=0).astype(BF16)
        x = jnp.dot(l2_ref[...], rhs, preferred_element_type=F32)
        for h in range(FFT_COLS // 128):
            o32_ref[h, pl.ds(k_lo, n2, stride=FFT_SPLIT), :] = x[:, 128 * h:128 * (h + 1)]
    for h in range(FFT_COLS // 128):
        o_ref[:, 128 * h:128 * (h + 1)] = o32_ref[h].astype(o_ref.dtype)


def _fourier_fft(proj, w_mix, *, batch, seq, col0):
    g, c, _ = w_mix.shape
    n2 = seq // FFT_SPLIT
    assert seq % FFT_SPLIT == 0 and FFT_COLS % c == 0 and (g * c) % FFT_COLS == 0 and col0 % FFT_COLS == 0
    blk0 = col0 // FFT_COLS
    scale = float(1.0 / np.sqrt(seq * c))
    cos_c, sin_c = _dft_tables(c, 8)
    cos_n, sin_n = _dft_tables(n2, 32)
    l2 = jnp.concatenate([cos_n, sin_n], axis=1).astype(BF16)
    k_lo = jnp.arange(1, FFT_SPLIT, dtype=jnp.int32)[:, None]
    ang = (k_lo * jnp.arange(n2, dtype=jnp.int32)[None, :]).astype(F32) * (2.0 * np.pi / seq)
    twc = jnp.broadcast_to(jnp.cos(ang)[:, :, None], (FFT_SPLIT - 1, n2, 128))
    tws = jnp.broadcast_to(jnp.sin(ang)[:, :, None], (FFT_SPLIT - 1, n2, 128))
    const2 = lambda b, j: (0, 0)
    const3 = lambda b, j: (0, 0, 0)
    return pl.pallas_call(
        functools.partial(_fourier_fft_kernel, scale=scale),
        grid=(batch, g * c // FFT_COLS),
        in_specs=[
            pl.BlockSpec((seq, FFT_COLS), lambda b, j: (b, blk0 + j)),
            pl.BlockSpec((c, c), const2),
            pl.BlockSpec((c, c), const2),
            pl.BlockSpec((g, c, c), const3),
            pl.BlockSpec(l2.shape, const2),
            pl.BlockSpec(twc.shape, const3),
            pl.BlockSpec(tws.shape, const3),
        ],
        out_specs=pl.BlockSpec((seq, FFT_COLS), lambda b, j: (b, j)),
        out_shape=jax.ShapeDtypeStruct((batch * seq, g * c), BF16),
        scratch_shapes=[pltpu.VMEM((2, seq, FFT_COLS), F32),
                        pltpu.VMEM((FFT_COLS // 128, seq, 128), F32)],
        compiler_params=_params(("parallel", "arbitrary")),
        name="fourier",
    )(proj, cos_c.astype(BF16), sin_c.astype(BF16), w_mix.astype(BF16), l2, twc, tws)


def _dft_tables(n, split):
    hi = n // split
    s = jnp.arange(n, dtype=jnp.int32)[None, :]
    k1 = jnp.arange(hi, dtype=jnp.int32)[:, None]
    k0 = jnp.arange(split, dtype=jnp.int32)[:, None]
    a1 = ((k1 * s * split) % n).astype(F32) * (2.0 * np.pi / n)
    a0 = ((k0 * s) % n).astype(F32) * (2.0 * np.pi / n)
    c1, s1 = jnp.cos(a1)[:, None, :], jnp.sin(a1)[:, None, :]
    c0, s0 = jnp.cos(a0)[None, :, :], jnp.sin(a0)[None, :, :]
    cos_m = (c1 * c0 - s1 * s0).reshape(n, n)
    sin_m = (s1 * c0 + c1 * s0).reshape(n, n)
    return cos_m, sin_m


PIECES_P = 3
COL_DIR = PIECES_P * HEAD_ROWS
SEL_COLS = HEADS_PER_GROUP * CHUNK
ROW_Q, ROW_YS, ROW_W, ROW_CD = 0, 1, 2, 3
ROW_TABLES = 4
LOG2E = float(np.log2(np.e))
CONV_TAPS = 5
WIN_ROWS = CHUNK + 2 * HALO


def _selection_matrices():
    sel = np.zeros((2, CHUNK, SEL_COLS), np.float32)
    for d in range(2):
        for r in range(HEADS_PER_GROUP):
            for piece in range(PIECES_P):
                sel[d, d * COL_DIR + piece * HEAD_ROWS + r, r * CHUNK:(r + 1) * CHUNK] = 1.0
    return sel


def _shift_matrix():
    taps = [k for k in range(CONV_TAPS) if k != CONV_TAPS // 2]
    m = np.zeros((len(taps) * CHUNK, WIN_ROWS), np.float32)
    for i, k in enumerate(taps):
        for t in range(CHUNK):
            m[i * CHUNK + t, HALO + t + k - CONV_TAPS // 2] = 1.0
    return m


def _softplus(v):
    return jnp.maximum(v, 0.0) + jnp.log1p(jnp.exp(-jnp.abs(v)))


def _bf16_pieces(v, n):
    out = []
    for _ in range(n - 1):
        hi = v.astype(BF16).astype(F32)
        out.append(hi)
        v = v - hi
    out.append(v)
    return out


def _ssd_kernel(z_ref, x_ref, b_ref, c_ref, dt_ref, cwx_ref, cwb_ref, cwc_ref, cbx_ref, cbb_ref, cbc_ref,
                hp_ref, d_ref, nw_ref, sel_ref, shift_ref, o_ref,
                xc_ref, xt_ref, colq_ref, rowq_ref, y_ref, st_ref, pb_ref, cbm_ref):
    seq = x_ref.shape[0]
    nchunks = seq // CHUNK
    half = nchunks // 2
    xbc_rows = lambda start, n: jnp.concatenate(
        [ref[pl.ds(start, n), :] for ref in (x_ref, b_ref, c_ref)], axis=1)
    conv_w = jnp.concatenate([cwx_ref[...], cwb_ref[...], cwc_ref[...]], axis=1)
    conv_b = jnp.concatenate([cbx_ref[...], cbb_ref[...], cbc_ref[...]], axis=1)
    li = lax.broadcasted_iota(jnp.int32, (CHUNK, CHUNK), 0)
    si = lax.broadcasted_iota(jnp.int32, (CHUNK, CHUNK), 1)
    upper_b = jnp.where(li <= si, 1.0, 0.0).astype(BF16)
    lane_lo = si < SSM_HEAD_DIM
    lane_lo_row = lax.broadcasted_iota(jnp.int32, (1, 128), 1) < SSM_HEAD_DIM

    hp = hp_ref[...]
    bias_f, a_f = hp[:, 0:1], -jnp.exp(hp[:, 1:2])
    bias_b, a_b = hp[:, 2:3], -jnp.exp(hp[:, 3:4])

    def prep(c):
        r0 = pl.multiple_of(c * CHUNK, CHUNK)
        cur = xbc_rows(r0, CHUNK)
        p0 = pl.multiple_of(jnp.maximum(r0 - HALO, 0), HALO)
        n0 = pl.multiple_of(jnp.minimum(r0 + CHUNK, seq - HALO), HALO)
        prev = xbc_rows(p0, HALO)
        nxt = xbc_rows(n0, HALO)
        prev = jnp.where(c > 0, prev, jnp.zeros_like(prev))
        nxt = jnp.where(c < nchunks - 1, nxt, jnp.zeros_like(nxt))
        win = jnp.concatenate([prev, cur, nxt], axis=0)
        taps = jnp.dot(shift_ref[...], win, preferred_element_type=F32)
        mid = CONV_TAPS // 2
        acc = cur.astype(F32) * conv_w[mid:mid + 1, :] + conv_b
        for i, k in enumerate([k for k in range(CONV_TAPS) if k != mid]):
            acc = acc + taps[i * CHUNK:(i + 1) * CHUNK, :] * conv_w[k:k + 1, :]
        act = _silu(acc)
        xc_ref[pl.ds(r0, CHUNK), :] = act.astype(BF16)
        xt_ref[:, pl.ds(r0, CHUNK)] = jnp.concatenate(
            [jnp.transpose(act[:, 128 * j:128 * (j + 1)]) for j in range(GROUP_X // 128)], axis=0).astype(BF16)

    prep_unroll = min(16, nchunks)

    def prep_group(i, carry):
        for u in range(prep_unroll):
            prep(prep_unroll * i + u)
        return carry

    lax.fori_loop(0, nchunks // prep_unroll, prep_group, 0)

    trows = nchunks * HEAD_ROWS
    dt_all = dt_ref[...]
    dtr = jnp.concatenate([dt_all[:, c * CHUNK:(c + 1) * CHUNK] for c in range(nchunks)], axis=0)
    per_row = lambda col: jnp.concatenate([jnp.broadcast_to(col, (HEAD_ROWS, CHUNK))] * nchunks, axis=0)
    lane_const = lambda v: jnp.broadcast_to(v, (trows, CHUNK))

    def cumsum(v):
        stacked = jnp.concatenate(_bf16_pieces(v, 3), axis=0).astype(BF16)
        out = jnp.dot(stacked, upper_b, preferred_element_type=F32)
        return (out[:trows] + out[trows:2 * trows]) + out[2 * trows:]

    dt_f = _softplus(dtr + per_row(bias_f))
    dt_b = _softplus(dtr + per_row(bias_b))
    adt_f = dt_f * per_row(a_f)
    adt_b = dt_b * per_row(a_b)
    cum_f = cumsum(adt_f)
    cum_b = cumsum(adt_b)
    tot_f = cum_f[:, CHUNK - 1:CHUNK]
    tot_b = cum_b[:, CHUNK - 1:CHUNK]
    ex_b = cum_b - adt_b
    p2_f = cum_f * LOG2E
    p2_b = -ex_b * LOG2E
    tables = [p2_f - jnp.log(dt_f) * LOG2E, jnp.zeros((trows, CHUNK), F32),
              jnp.exp(tot_f - cum_f) * dt_f, lane_const(jnp.exp(tot_f)),
              p2_b - jnp.log(dt_b) * LOG2E, lane_const(tot_b * LOG2E),
              jnp.exp(ex_b) * dt_b, lane_const(jnp.exp(tot_b))]
    for q, tab in enumerate(tables):
        rowq_ref[q] = tab
    pieces = _bf16_pieces(p2_f, PIECES_P) + _bf16_pieces(p2_b, PIECES_P)
    pad = jnp.zeros((CHUNK - 2 * COL_DIR, CHUNK), F32)
    for c in range(nchunks):
        rows = [p[c * HEAD_ROWS:(c + 1) * HEAD_ROWS, :] for p in pieces] + [pad]
        colq_ref[c * CHUNK:(c + 1) * CHUNK, :] = jnp.transpose(jnp.concatenate(rows, axis=0)).astype(BF16)

    nt = (((1,), (1,)), ((), ()))

    def stage_ahead(c, direction):
        r0 = pl.multiple_of(c * CHUNK, CHUNK)
        pb_ref[direction] = jnp.dot(colq_ref[pl.ds(r0, CHUNK), :], sel_ref[direction],
                                    preferred_element_type=F32)
        cbm_ref[direction] = lax.dot_general(
            xc_ref[pl.ds(r0, CHUNK), GROUP_X + SSM_STATE:], xc_ref[pl.ds(r0, CHUNK), GROUP_X:GROUP_X + SSM_STATE],
            nt, preferred_element_type=F32)

    def chunk(c, direction):
        r0 = pl.multiple_of(c * CHUNK, CHUNK)
        xcb = xc_ref[pl.ds(r0, CHUNK), :]
        xs = xcb[:, :GROUP_X]
        bm = xcb[:, GROUP_X:GROUP_X + SSM_STATE]
        cm = xcb[:, GROUP_X + SSM_STATE:]
        r8 = pl.multiple_of(c * HEAD_ROWS, HEAD_ROWS)
        tabs = [rowq_ref[direction * ROW_TABLES + q, pl.ds(r8, HEAD_ROWS), :] for q in range(ROW_TABLES)]
        row = lambda q, r: tabs[q][r:r + 1, :]
        pb = pb_ref[direction]
        cb = cbm_ref[direction]
        step = 1 if direction == 0 else -1
        stage_ahead(jnp.clip(c + step, 0, nchunks - 1), direction)
        mask = (li >= si) if direction == 0 else (si >= li)

        state_t = st_ref[direction]
        y_off = lax.dot_general(cm, state_t.astype(BF16), nt, preferred_element_type=F32)

        xs_lo = jnp.where(lane_lo_row, 1.0, 0.0).astype(BF16)
        xs_hi = jnp.where(lane_lo_row, 0.0, 1.0).astype(BF16)
        tiles = []
        for j in range(HEADS_PER_GROUP // 2):
            xp = xs[:, 128 * j:128 * (j + 1)]
            pa = pb[:, 2 * j * CHUNK:(2 * j + 1) * CHUNK]
            pc = pb[:, (2 * j + 1) * CHUNK:(2 * j + 2) * CHUNK]
            ys = jnp.exp2(jnp.where(lane_lo, pa, pc)
                          + jnp.where(lane_lo_row, row(ROW_YS, 2 * j), row(ROW_YS, 2 * j + 1)))
            ma = (cb * jnp.exp2(jnp.where(mask, pa - row(ROW_Q, 2 * j), -jnp.inf))).astype(BF16)
            mc = (cb * jnp.exp2(jnp.where(mask, pc - row(ROW_Q, 2 * j + 1), -jnp.inf))).astype(BF16)
            tiles.append(y_off[:, 128 * j:128 * (j + 1)] * ys
                         + jnp.dot(ma, xp * xs_lo, preferred_element_type=F32)
                         + jnp.dot(mc, xp * xs_hi, preferred_element_type=F32))
        y = jnp.concatenate(tiles, axis=1)

        per_head = lambda off, dtype=F32: jnp.concatenate(
            [jnp.broadcast_to(row(off, r).astype(dtype), (SSM_HEAD_DIM, CHUNK)) for r in range(HEADS_PER_GROUP)],
            axis=0)
        xw = xt_ref[:, pl.ds(r0, CHUNK)] * per_head(ROW_W, BF16)
        new = jnp.dot(xw, bm, preferred_element_type=F32)
        st_ref[direction] = state_t * per_head(ROW_CD) + new
        return r0, xs, y

    def finish(r0, xs, y):
        y = y + y_ref[pl.ds(r0, CHUNK), :] + d_ref[...] * xs.astype(F32)
        y = y * _silu(z_ref[pl.ds(r0, CHUNK), :].astype(F32))
        ms = jnp.mean(y * y, axis=-1, keepdims=True)
        o_ref[pl.ds(r0, CHUNK), :] = (y * lax.rsqrt(ms + NORM_EPS) * nw_ref[...]).astype(o_ref.dtype)

    st_ref[...] = jnp.zeros_like(st_ref)

    scan_unroll = next(u for u in (8, 4, 2, 1) if half % u == 0)

    def first_visits(t, carry):
        for u in range(scan_unroll):
            i = scan_unroll * t + u
            for c, direction in ((i, 0), (nchunks - 1 - i, 1)):
                r0, _, y = chunk(c, direction)
                y_ref[pl.ds(r0, CHUNK), :] = y
        return carry

    def second_visits(t, carry):
        for u in range(scan_unroll):
            i = half + scan_unroll * t + u
            for c, direction in ((i, 0), (nchunks - 1 - i, 1)):
                finish(*chunk(c, direction))
        return carry

    stage_ahead(0, 0)
    stage_ahead(nchunks - 1, 1)
    lax.fori_loop(0, half // scan_unroll, first_visits, 0)
    lax.fori_loop(0, half // scan_unroll, second_visits, 0)


def _ssd(proj, dt_t, conv_w, conv_b, head_params, d_exp, norm_w, *, batch, seq):
    g = SSM_GROUPS
    width = conv_w.shape[0]
    assert width == CONV_TAPS and (seq // CHUNK) % 2 == 0
    sel = jnp.asarray(_selection_matrices(), BF16)
    shift = jnp.asarray(_shift_matrix(), BF16)
    x_blk = g
    b_blk = 2 * g * GROUP_X // SSM_STATE
    c_blk = b_blk + g
    cwb_blk = g * GROUP_X // SSM_STATE
    cwc_blk = cwb_blk + g
    return pl.pallas_call(
        _ssd_kernel,
        grid=(batch, g),
        in_specs=[
            pl.BlockSpec((seq, GROUP_X), lambda b, k: (b, k)),
            pl.BlockSpec((seq, GROUP_X), lambda b, k: (b, x_blk + k)),
            pl.BlockSpec((seq, SSM_STATE), lambda b, k: (b, b_blk + k)),
            pl.BlockSpec((seq, SSM_STATE), lambda b, k: (b, c_blk + k)),
            pl.BlockSpec((HEAD_ROWS, seq), lambda b, k: (k, b)),
            pl.BlockSpec((width, GROUP_X), lambda b, k: (0, k)),
            pl.BlockSpec((width, SSM_STATE), lambda b, k: (0, cwb_blk + k)),
            pl.BlockSpec((width, SSM_STATE), lambda b, k: (0, cwc_blk + k)),
            pl.BlockSpec((1, GROUP_X), lambda b, k: (0, k)),
            pl.BlockSpec((1, SSM_STATE), lambda b, k: (0, cwb_blk + k)),
            pl.BlockSpec((1, SSM_STATE), lambda b, k: (0, cwc_blk + k)),
            pl.BlockSpec((None, HEAD_ROWS, 4), lambda b, k: (k, 0, 0)),
            pl.BlockSpec((None, 1, GROUP_X), lambda b, k: (k, 0, 0)),
            pl.BlockSpec((None, 1, GROUP_X), lambda b, k: (k, 0, 0)),
            pl.BlockSpec(sel.shape, lambda b, k: (0, 0, 0)),
            pl.BlockSpec(shift.shape, lambda b, k: (0, 0)),
        ],
        out_specs=pl.BlockSpec((seq, GROUP_X), lambda b, k: (b, k)),
        out_shape=jax.ShapeDtypeStruct((batch * seq, g * GROUP_X), BF16),
        scratch_shapes=[
            pltpu.VMEM((seq, GROUP_XBC), BF16),
            pltpu.VMEM((GROUP_X, seq), BF16),
            pltpu.VMEM((seq, CHUNK), BF16),
            pltpu.VMEM((2 * ROW_TABLES, seq // CHUNK * HEAD_ROWS, CHUNK), F32),
            pltpu.VMEM((seq, GROUP_X), F32),
            pltpu.VMEM((2, GROUP_X, SSM_STATE), F32),
            pltpu.VMEM((2, CHUNK, SEL_COLS), F32),
            pltpu.VMEM((2, CHUNK, CHUNK), F32),
        ],
        compiler_params=_params(("parallel", "arbitrary")),
        name="ssd",
    )(proj, proj, proj, proj, dt_t, conv_w, conv_w, conv_w, conv_b, conv_b, conv_b,
      head_params, d_exp, norm_w, sel, shift)


def _out_proj_kernel(x_ref, a_ref, b_ref, w_ref, o_ref):
    ka = a_ref.shape[1]
    acc = jnp.dot(a_ref[...], w_ref[:ka, :], preferred_element_type=F32)
    acc = acc + jnp.dot(b_ref[...], w_ref[ka:, :], preferred_element_type=F32)
    o_ref[...] = x_ref[...] + acc


def _out_proj(x2, a_out, b_out, w, *, tm, tn):
    t, d = x2.shape
    ka, kb = a_out.shape[1], b_out.shape[1]
    resident = pl.Buffered(1) if tn == d else None
    return pl.pallas_call(
        _out_proj_kernel,
        grid=(t // tm, d // tn),
        in_specs=[
            pl.BlockSpec((tm, tn), lambda i, j: (i, j)),
            pl.BlockSpec((tm, ka), lambda i, j: (i, 0)),
            pl.BlockSpec((tm, kb), lambda i, j: (i, 0)),
            pl.BlockSpec((ka + kb, tn), lambda i, j: (0, j), pipeline_mode=resident),
        ],
        out_specs=pl.BlockSpec((tm, tn), lambda i, j: (i, j)),
        out_shape=jax.ShapeDtypeStruct((t, d), F32),
        compiler_params=_params(("parallel", "arbitrary")),
        name="out_proj",
    )(x2, a_out, b_out, w)


FFN_HALO = HALO


def _ffn_up_kernel(x_ref, xp_ref, xn_ref, g_ref, wg_ref, wv_ref, cwg_ref, cwv_ref, cbg_ref, cbv_ref,
                   o_ref, h_ref, *, tiles_per_seq):
    tm = x_ref.shape[0]
    i = pl.program_id(0)

    def norm(v):
        var = jnp.mean(v * v, axis=-1, keepdims=True)
        return v * lax.rsqrt(var + NORM_EPS) * g_ref[...]

    @pl.when(pl.program_id(1) == 0)
    def _():
        first = (i % tiles_per_seq) == 0
        last = (i % tiles_per_seq) == tiles_per_seq - 1
        h_ref[0:FFN_HALO, :] = jnp.where(first, 0.0, norm(xp_ref[...])).astype(BF16)
        h_ref[FFN_HALO + tm:, :] = jnp.where(last, 0.0, norm(xn_ref[...])).astype(BF16)

        def body(r, carry):
            r0 = pl.multiple_of(r * 128, 128)
            h_ref[pl.ds(FFN_HALO + r0, 128), :] = norm(x_ref[pl.ds(r0, 128), :]).astype(BF16)
            return carry
        lax.fori_loop(0, tm // 128, body, 0)

    rows = tm + 2 * FFN_HALO

    def conv(w_ref, cw_ref, cb_ref):
        up = jnp.dot(h_ref[...], w_ref[...].astype(BF16), preferred_element_type=F32)
        acc = up[FFN_HALO:FFN_HALO + tm, :] * cw_ref[1:2, :] + cb_ref[...]
        acc = acc + pltpu.roll(up, 1, axis=0)[FFN_HALO:FFN_HALO + tm, :] * cw_ref[0:1, :]
        acc = acc + pltpu.roll(up, rows - 1, axis=0)[FFN_HALO:FFN_HALO + tm, :] * cw_ref[2:3, :]
        return acc

    gate = conv(wg_ref, cwg_ref, cbg_ref)
    val = conv(wv_ref, cwv_ref, cbv_ref)
    o_ref[...] = (_silu(gate) * val).astype(o_ref.dtype)


def _ffn_up(x1, gain, w_up, conv_w, conv_b, *, seq, tm, tn):
    t, d = x1.shape
    f = w_up.shape[1] // 2
    nj = f // tn
    tiles_per_seq = seq // tm
    hb = tm // FFN_HALO
    nhb = t // FFN_HALO
    return pl.pallas_call(
        functools.partial(_ffn_up_kernel, tiles_per_seq=tiles_per_seq),
        grid=(t // tm, nj),
        in_specs=[
            pl.BlockSpec((tm, d), lambda i, j: (i, 0)),
            pl.BlockSpec((FFN_HALO, d), lambda i, j: (jnp.maximum(i * hb - 1, 0), 0)),
            pl.BlockSpec((FFN_HALO, d), lambda i, j: (jnp.minimum((i + 1) * hb, nhb - 1), 0)),
            pl.BlockSpec((1, d), lambda i, j: (0, 0)),
            pl.BlockSpec((d, tn), lambda i, j: (0, j)),
            pl.BlockSpec((d, tn), lambda i, j: (0, j + nj)),
            pl.BlockSpec((3, tn), lambda i, j: (0, j)),
            pl.BlockSpec((3, tn), lambda i, j: (0, j + nj)),
            pl.BlockSpec((1, tn), lambda i, j: (0, j)),
            pl.BlockSpec((1, tn), lambda i, j: (0, j + nj)),
        ],
        out_specs=pl.BlockSpec((tm, tn), lambda i, j: (i, j)),
        out_shape=jax.ShapeDtypeStruct((t, f), BF16),
        scratch_shapes=[pltpu.VMEM((tm + 2 * FFN_HALO, d), BF16)],
        compiler_params=_params(("parallel", "arbitrary")),
        name="ffn_up",
    )(x1, x1, x1, gain, w_up, w_up, conv_w, conv_w, conv_b, conv_b)


def _ffn_down_kernel(x_ref, a_ref, w_ref, g_ref, o_ref):
    v = x_ref[...] + jnp.dot(a_ref[...], w_ref[...], preferred_element_type=F32)
    var = jnp.mean(v * v, axis=-1, keepdims=True)
    o_ref[...] = v * lax.rsqrt(var + NORM_EPS) * g_ref[...]


def _ffn_down(x1, act, w_down, gain, *, tm):
    t, d = x1.shape
    f = act.shape[1]
    return pl.pallas_call(
        _ffn_down_kernel,
        grid=(t // tm,),
        in_specs=[
            pl.BlockSpec((tm, d), lambda i: (i, 0)),
            pl.BlockSpec((tm, f), lambda i: (i, 0)),
            pl.BlockSpec((f, d), lambda i: (0, 0), pipeline_mode=pl.Buffered(1)),
            pl.BlockSpec((1, d), lambda i: (0, 0)),
        ],
        out_specs=pl.BlockSpec((tm, d), lambda i: (i, 0)),
        out_shape=jax.ShapeDtypeStruct((t, d), F32),
        compiler_params=_params(("parallel",)),
        name="ffn_down",
    )(x1, act, w_down, gain)


def _layer(x2, batch, seq, norm_mix_w, w_in, fourier_w, ssm_conv_w, ssm_conv_b, dt_bias_fwd, a_log_fwd,
           dt_bias_bwd, a_log_bwd, ssm_d, ssm_norm_w, w_out, norm_ffn_w, w_up, ffn_conv_w, ffn_conv_b,
           w_down, final_gain, tiles):
    d = x2.shape[1]
    g, r = SSM_GROUPS, HEADS_PER_GROUP
    fw = fourier_w.shape[0] * fourier_w.shape[1]
    sw = g * GROUP_X
    gn = g * SSM_STATE
    main = fw + 2 * sw + 2 * gn
    assert fw == tiles["in_tn"], "u must be exactly the first weight tile for the rotated read"

    wdt = jnp.pad(w_in[:, main:].reshape(d, g, r), ((0, 0), (0, 0), (0, HEAD_ROWS - r)))
    wdt = jnp.pad(wdt.reshape(d, g * HEAD_ROWS), ((0, 0), (0, 128 - g * HEAD_ROWS))).astype(BF16)

    proj, dt_t = _in_proj(x2, norm_mix_w[None, :], w_in.astype(BF16), wdt, n=main, nh=g * HEAD_ROWS,
                          tm=tiles["in_tm"], tn=tiles["in_tn"])

    a_out = _fourier_fft(proj, fourier_w, batch=batch, seq=seq, col0=main - fw)

    hp = jnp.stack([dt_bias_fwd, a_log_fwd, dt_bias_bwd, a_log_bwd], axis=-1).reshape(g, r, 4)
    hp = jnp.pad(hp, ((0, 0), (0, HEAD_ROWS - r), (0, 0)))
    d_exp = jnp.repeat(ssm_d, SSM_HEAD_DIM).reshape(g, 1, GROUP_X)
    b_out = _ssd(proj, dt_t, ssm_conv_w, ssm_conv_b[None, :], hp, d_exp, ssm_norm_w.reshape(g, 1, GROUP_X),
                 batch=batch, seq=seq)

    x1 = _out_proj(x2, a_out, b_out, w_out.astype(BF16), tm=tiles["o_tm"], tn=min(tiles["o_tn"], d))

    act = _ffn_up(x1, norm_ffn_w[None, :], w_up, ffn_conv_w, ffn_conv_b[None, :],
                  seq=seq, tm=tiles["u_tm"], tn=tiles["u_tn"])
    return _ffn_down(x1, act, w_down.astype(BF16), final_gain[None, :], tm=tiles["d_tm"])


TILES = dict(in_tm=1024, in_tn=1024, o_tm=512, o_tn=2048, u_tm=1024, u_tn=512, d_tm=256)


def kernel(x, norm_mix_w, w_in, fourier_w, ssm_conv_w, ssm_conv_b, dt_bias_fwd, a_log_fwd, dt_bias_bwd,
           a_log_bwd, ssm_d, ssm_norm_w, w_out, norm_ffn_w, w_up, ffn_conv_w, ffn_conv_b, w_down,
           norm_final_w):
    batch, seq, d = x.shape
    assert norm_mix_w.shape[0] == 1, "one layer"
    out = _layer(x.reshape(batch * seq, d), batch, seq, norm_mix_w[0], w_in[0], fourier_w[0], ssm_conv_w[0],
                 ssm_conv_b[0], dt_bias_fwd[0], a_log_fwd[0], dt_bias_bwd[0], a_log_bwd[0], ssm_d[0],
                 ssm_norm_w[0], w_out[0], norm_ffn_w[0], w_up[0], ffn_conv_w[0], ffn_conv_b[0], w_down[0],
                 norm_final_w, TILES)
    return out.reshape(batch, seq, d)
```
